```python
import math
import jax, jax.numpy as jnp
from jax import lax
import numpy as np

D_MODEL = 2048
BATCH = 16
SEQ = 2048
DEPTH = 4

N_META = 16
BLOCK_Q = 128
N_A_LAYERS = DEPTH // 2
N_B_LAYERS = DEPTH - N_A_LAYERS
DA_HEAD_DIM = 64
DA_N_HEADS = D_MODEL // (2 * DA_HEAD_DIM)
SB_HEAD_DIM = 128
SB_N_HEADS = D_MODEL // SB_HEAD_DIM
D_FF = 4 * D_MODEL
RMS_EPS = 1e-6
LAMBDA_STD = 0.1

kernel_name = "yoco_diffattn_stickbreaking_hybrid"


def rmsnorm(x, g):
    xf = x.astype(jnp.float32)
    y = xf * lax.rsqrt(jnp.mean(xf * xf, axis=-1, keepdims=True) + RMS_EPS)
    return (y * g.astype(jnp.float32)).astype(x.dtype)


def alibi_slopes(n_heads):
    return jnp.asarray(np.array([2.0 ** (-8.0 * (i + 1) / n_heads) for i in range(n_heads)], dtype=np.float32))


def query_blocks(total_len):
    bounds = [(0, N_META)]
    for s in range(N_META, total_len, BLOCK_Q):
        bounds.append((s, min(s + BLOCK_Q, total_len)))
    return bounds


def diff_attention(h, w_qkv, w_o, lam_q1, lam_k1, lam_q2, lam_k2, subln_g, lambda_init):
    b, L, _ = h.shape
    q, k, v = jnp.split(h @ w_qkv, 3, axis=-1)
    q = q.reshape(b, L, DA_N_HEADS, 2, DA_HEAD_DIM)
    k = k.reshape(b, L, DA_N_HEADS, 2, DA_HEAD_DIM)
    v = v.reshape(b, L, DA_N_HEADS, 2 * DA_HEAD_DIM)
    f32 = jnp.float32
    lam = (jnp.exp(jnp.sum(lam_q1.astype(f32) * lam_k1.astype(f32)))
           - jnp.exp(jnp.sum(lam_q2.astype(f32) * lam_k2.astype(f32))) + lambda_init)
    slopes = alibi_slopes(DA_N_HEADS)
    scale = DA_HEAD_DIM ** -0.5
    pos = jnp.arange(L)
    outs = []
    for s0, s1 in query_blocks(L):
        qb = q[:, s0:s1].astype(f32) * scale
        kb = k[:, :s1].astype(f32)
        scores = jnp.einsum('bqhcd,bkhcd->bhcqk', qb, kb)
        dist = (pos[s0:s1, None] - pos[None, :s1]).astype(f32)
        bias = jnp.where(dist[None] >= 0, -slopes[:, None, None] * dist[None], -jnp.inf)
        p = jax.nn.softmax(scores + bias[None, :, None], axis=-1)
        attn = p[:, :, 0] - lam * p[:, :, 1]
        outs.append(jnp.einsum('bhqk,bkhe->bqhe', attn, v[:, :s1].astype(f32)))
    o = jnp.concatenate(outs, axis=1)
    o = rmsnorm(o, subln_g) * (1.0 - lambda_init)
    return o.reshape(b, L, D_MODEL).astype(h.dtype) @ w_o


def stick_breaking_attention(h, w_q, k, v, w_o):
    b, L, _ = h.shape
    f32 = jnp.float32
    q = (h @ w_q).reshape(b, L, SB_N_HEADS, SB_HEAD_DIM)
    scale = SB_HEAD_DIM ** -0.5
    pos = jnp.arange(L)
    outs = []
    for s0, s1 in query_blocks(L):
        qb = q[:, s0:s1].astype(f32) * scale
        z = jnp.einsum('bqhd,bkhd->bhqk', qb, k[:, :s1].astype(f32))
        strict = pos[s0:s1, None] > pos[None, :s1]
        log_beta = jax.nn.log_sigmoid(z)
        log_1m_beta = jnp.where(strict, jax.nn.log_sigmoid(-z), 0.0)
        log_keep = lax.cumsum(log_1m_beta, axis=3, reverse=True) - log_1m_beta
        attn = jnp.where(strict, jnp.exp(log_beta + log_keep), 0.0)
        outs.append(jnp.einsum('bhqk,bkhd->bqhd', attn, v[:, :s1].astype(f32)))
    o = jnp.concatenate(outs, axis=1)
    return o.reshape(b, L, D_MODEL).astype(h.dtype) @ w_o


def sq_relu_mlp(h, w_up, w_down):
    a = jax.nn.relu(h @ w_up)
    return (a * a) @ w_down


def setup_inputs(seed: int = 0) -> dict:
    key = jax.random.key(seed)
    ks = jax.random.split(key, 20)
    D, F = D_MODEL, D_FF
    nrm = jax.random.normal
    return {
        "x": nrm(ks[0], (BATCH, SEQ, D), jnp.float32),
        "meta_tokens": nrm(ks[1], (N_META, D), jnp.float32),
        "attn_norm_g": 1.0 + 0.02 * nrm(ks[2], (DEPTH, D), jnp.float32),
        "mlp_norm_g": 1.0 + 0.02 * nrm(ks[3], (DEPTH, D), jnp.float32),
        "da_w_qkv": nrm(ks[4], (N_A_LAYERS, D, 3 * D), jnp.float32) * D ** -0.5,
        "da_w_o": nrm(ks[5], (N_A_LAYERS, D, D), jnp.float32) * D ** -0.5,
        "da_lambda_q1": LAMBDA_STD * nrm(ks[6], (N_A_LAYERS, DA_HEAD_DIM), jnp.float32),
        "da_lambda_k1": LAMBDA_STD * nrm(ks[7], (N_A_LAYERS, DA_HEAD_DIM), jnp.float32),
        "da_lambda_q2": LAMBDA_STD * nrm(ks[8], (N_A_LAYERS, DA_HEAD_DIM), jnp.float32),
        "da_lambda_k2": LAMBDA_STD * nrm(ks[9], (N_A_LAYERS, DA_HEAD_DIM), jnp.float32),
        "da_subln_g": 1.0 + 0.02 * nrm(ks[10], (N_A_LAYERS, 2 * DA_HEAD_DIM), jnp.float32),
        "kv_norm_g": 1.0 + 0.02 * nrm(ks[11], (D,), jnp.float32),
        "sb_w_k": nrm(ks[12], (D, D), jnp.float32) * D ** -0.5,
        "sb_w_v": nrm(ks[13], (D, D), jnp.float32) * D ** -0.5,
        "sb_w_q": nrm(ks[14], (N_B_LAYERS, D, D), jnp.float32) * D ** -0.5,
        "sb_w_o": nrm(ks[15], (N_B_LAYERS, D, D), jnp.float32) * D ** -0.5,
        "mlp_w_up": nrm(ks[16], (DEPTH, D, F), jnp.float32) * D ** -0.5,
        "mlp_w_down": nrm(ks[17], (DEPTH, F, D), jnp.float32) * F ** -0.5,
        "final_norm_g": 1.0 + 0.02 * nrm(ks[18], (D,), jnp.float32),
    }


def reference(x, meta_tokens, attn_norm_g, mlp_norm_g, da_w_qkv, da_w_o,
              da_lambda_q1, da_lambda_k1, da_lambda_q2, da_lambda_k2, da_subln_g,
              kv_norm_g, sb_w_k, sb_w_v, sb_w_q, sb_w_o, mlp_w_up, mlp_w_down,
              final_norm_g):
    b = x.shape[0]
    meta = jnp.broadcast_to(meta_tokens.astype(x.dtype)[None], (b, N_META, D_MODEL))
    h = jnp.concatenate([meta, x], axis=1)
    L = h.shape[1]
    k_shared = v_shared = None
    for i in range(DEPTH):
        if i < N_A_LAYERS:
            lambda_init = 0.8 - 0.6 * math.exp(-0.3 * i)
            h = h + diff_attention(rmsnorm(h, attn_norm_g[i]), da_w_qkv[i], da_w_o[i],
                                   da_lambda_q1[i], da_lambda_k1[i], da_lambda_q2[i],
                                   da_lambda_k2[i], da_subln_g[i], lambda_init)
        else:
            if i == N_A_LAYERS:
                hk = rmsnorm(h, kv_norm_g)
                k_shared = (hk @ sb_w_k).reshape(b, L, SB_N_HEADS, SB_HEAD_DIM)
                v_shared = (hk @ sb_w_v).reshape(b, L, SB_N_HEADS, SB_HEAD_DIM)
            j = i - N_A_LAYERS
            h = h + stick_breaking_attention(rmsnorm(h, attn_norm_g[i]), sb_w_q[j],
                                             k_shared, v_shared, sb_w_o[j])
        h = h + sq_relu_mlp(rmsnorm(h, mlp_norm_g[i]), mlp_w_up[i], mlp_w_down[i])
    return rmsnorm(h, final_norm_g)[:, N_META:]
```

```python
import functools
import math

import jax
import jax.numpy as jnp
import numpy as np
from jax import lax
from jax.experimental import pallas as pl
from jax.experimental.pallas import tpu as pltpu

F32 = jnp.float32
BF16 = jnp.bfloat16

RMS_EPS = 1e-6
DA_HEAD_DIM = 64
SB_HEAD_DIM = 128
ATTN_BLOCK = 256
BF16_SUBLANES = 16
VMEM_LIMIT_BYTES = 56 * 1024 * 1024


def _pick_tile(n, target, multiple=BF16_SUBLANES):
    best = None
    for t in range(multiple, min(n, target) + 1, multiple):
        if n % t == 0:
            best = t
    if best is None:
        raise ValueError(f"no tile for {n=} {target=}")
    return best


def _rmsnorm_rows(x, g):
    ms = jnp.mean(x * x, axis=-1, keepdims=True)
    return x * lax.rsqrt(ms + RMS_EPS) * g


def _norm_matmul_kernel(h_ref, g_ref, w_ref, o_ref, xn_ref, *, out_scale):
    @pl.when(pl.program_id(1) == 0)
    def _():
        xn_ref[...] = _rmsnorm_rows(h_ref[...], g_ref[...]).astype(BF16)

    acc = jnp.dot(xn_ref[...], w_ref[...], preferred_element_type=F32)
    if out_scale != 1.0:
        acc = acc * out_scale
    o_ref[...] = acc.astype(o_ref.dtype)


def norm_matmul(h, g, w, *, out_scale=1.0, tm_target=768, tn_target=1024):
    m, d = h.shape
    n = w.shape[1]
    tm = _pick_tile(m, tm_target)
    tn = _pick_tile(n, tn_target, 128)
    return pl.pallas_call(
        functools.partial(_norm_matmul_kernel, out_scale=out_scale),
        out_shape=jax.ShapeDtypeStruct((m, n), BF16),
        grid=(m // tm, n // tn),
        in_specs=[
            pl.BlockSpec((tm, d), lambda i, j: (i, 0)),
            pl.BlockSpec((1, d), lambda i, j: (0, 0)),
            pl.BlockSpec((d, tn), lambda i, j: (0, j)),
        ],
        out_specs=pl.BlockSpec((tm, tn), lambda i, j: (i, j)),
        scratch_shapes=[pltpu.VMEM((tm, d), BF16)],
        compiler_params=pltpu.CompilerParams(
            dimension_semantics=("parallel", "arbitrary"),
            vmem_limit_bytes=VMEM_LIMIT_BYTES),
        name="norm_matmul",
    )(h, g.reshape(1, d), w)


def _matmul_residual_kernel(h_ref, o_ref, w_ref, out_ref):
    out_ref[...] = h_ref[...] + jnp.dot(o_ref[...], w_ref[...],
                                        preferred_element_type=F32)


def matmul_residual(h, o, w, *, tm_target=768, tn_target=1024):
    m, d = h.shape
    k = o.shape[1]
    tm = _pick_tile(m, tm_target)
    tn = _pick_tile(d, tn_target, 128)
    return pl.pallas_call(
        _matmul_residual_kernel,
        out_shape=jax.ShapeDtypeStruct((m, d), F32),
        grid=(m // tm, d // tn),
        in_specs=[
            pl.BlockSpec((tm, tn), lambda i, j: (i, j)),
            pl.BlockSpec((tm, k), lambda i, j: (i, 0)),
            pl.BlockSpec((k, tn), lambda i, j: (0, j)),
        ],
        out_specs=pl.BlockSpec((tm, tn), lambda i, j: (i, j)),
        compiler_params=pltpu.CompilerParams(
            dimension_semantics=("parallel", "arbitrary"),
            vmem_limit_bytes=VMEM_LIMIT_BYTES),
        name="matmul_residual",
    )(h, o, w)


def _mlp_kernel(h_ref, g_ref, wu_ref, wd_ref, out_ref, xn_ref):
    @pl.when(pl.program_id(1) == 0)
    def _():
        x = h_ref[...]
        xn_ref[...] = _rmsnorm_rows(x, g_ref[...]).astype(BF16)
        out_ref[...] = x

    a = jnp.dot(xn_ref[...], wu_ref[...], preferred_element_type=F32)
    a = jnp.maximum(a, 0.0)
    a = (a * a).astype(BF16)
    out_ref[...] += jnp.dot(a, wd_ref[...], preferred_element_type=F32)


def mlp(h, g, w_up, w_down, *, tm_target=768, tf_target=512):
    m, d = h.shape
    f = w_up.shape[1]
    tm = _pick_tile(m, tm_target)
    tf = _pick_tile(f, tf_target, 128)
    return pl.pallas_call(
        _mlp_kernel,
        out_shape=jax.ShapeDtypeStruct((m, d), F32),
        grid=(m // tm, f // tf),
        in_specs=[
            pl.BlockSpec((tm, d), lambda i, j: (i, 0)),
            pl.BlockSpec((1, d), lambda i, j: (0, 0)),
            pl.BlockSpec((d, tf), lambda i, j: (0, j)),
            pl.BlockSpec((tf, d), lambda i, j: (j, 0)),
        ],
        out_specs=pl.BlockSpec((tm, d), lambda i, j: (i, 0)),
        scratch_shapes=[pltpu.VMEM((tm, d), BF16)],
        compiler_params=pltpu.CompilerParams(
            dimension_semantics=("parallel", "arbitrary"),
            vmem_limit_bytes=VMEM_LIMIT_BYTES),
        name="mlp",
    )(h, g.reshape(1, d), w_up, w_down)


def _final_norm_kernel(h_ref, g_ref, o_ref):
    o_ref[...] = _rmsnorm_rows(h_ref[...], g_ref[...])


def final_norm(h, g, *, tm_target=768):
    m, d = h.shape
    tm = _pick_tile(m, tm_target)
    return pl.pallas_call(
        _final_norm_kernel,
        out_shape=jax.ShapeDtypeStruct((m, d), F32),
        grid=(m // tm,),
        in_specs=[
            pl.BlockSpec((tm, d), lambda i: (i, 0)),
            pl.BlockSpec((1, d), lambda i: (0, 0)),
        ],
        out_specs=pl.BlockSpec((tm, d), lambda i: (i, 0)),
        compiler_params=pltpu.CompilerParams(
            dimension_semantics=("parallel",),
            vmem_limit_bytes=VMEM_LIMIT_BYTES),
        name="final_norm",
    )(h, g.reshape(1, d))


def _dot_nt(a, b):
    return lax.dot_general(a, b, (((1,), (1,)), ((), ())),
                           preferred_element_type=F32)


def _da_attn_kernel(slopes_ref, lq1_ref, lk1_ref, lq2_ref, lk2_ref, q_ref, k_ref,
                    v_ref, g_ref, o_ref, *, seq_len, block, lambda_init):
    hd = DA_HEAD_DIM
    slope = slopes_ref[pl.program_id(1)]
    lam = (jnp.exp(jnp.sum(lq1_ref[...] * lk1_ref[...], axis=-1, keepdims=True))
           - jnp.exp(jnp.sum(lq2_ref[...] * lk2_ref[...], axis=-1, keepdims=True))
           + lambda_init)
    scale = hd ** -0.5
    lane = lax.broadcasted_iota(jnp.int32, (1, 2 * hd), 1)
    first_map = lane < hd
    n_main = seq_len // block
    tail = seq_len - n_main * block

    def softmax_step(qq, carry, k0, nk, q0, nq, diagonal):
        m, l, acc = carry
        kb = k_ref[pl.ds(k0, nk), :]
        vb = v_ref[pl.ds(k0, nk), :]
        s = _dot_nt(qq, kb)
        col = lax.broadcasted_iota(jnp.int32, (1, nk), 1)
        s = s + slope * (col + (k0 - q0)).astype(F32)
        if diagonal:
            row = lax.broadcasted_iota(jnp.int32, (2 * nq, nk), 0)
            row = jnp.where(row >= nq, row - nq, row)
            s = jnp.where(col <= row, s, -jnp.inf)
        m_new = jnp.maximum(m, jnp.max(s, axis=-1, keepdims=True))
        alpha = jnp.exp(m - m_new)
        p = jnp.exp(s - m_new)
        l = alpha * l + jnp.sum(p, axis=-1, keepdims=True)
        acc = alpha * acc + jnp.dot(p.astype(BF16), vb, preferred_element_type=F32)
        return m_new, l, acc

    def q_block(q0, nq, n_full):
        q = (q_ref[pl.ds(q0, nq), :].astype(F32) * scale).astype(BF16)
        zero = jnp.zeros_like(q)
        qq = jnp.concatenate([jnp.where(first_map, q, zero),
                              jnp.where(first_map, zero, q)], axis=0)
        init = (jnp.full((2 * nq, 1), -jnp.inf, F32),
                jnp.zeros((2 * nq, 1), F32),
                jnp.zeros((2 * nq, 2 * hd), F32))

        def full_step(j, carry):
            k0 = pl.multiple_of(j * block, block)
            return softmax_step(qq, carry, k0, block, q0, nq, False)

        carry = lax.fori_loop(0, n_full, full_step, init)
        _, l, acc = softmax_step(qq, carry, q0, nq, q0, nq, True)
        o = acc / l
        o = o[:nq] - lam * o[nq:]
        o = _rmsnorm_rows(o, g_ref[...]) * (1.0 - lambda_init)
        o_ref[pl.ds(q0, nq), :] = o.astype(o_ref.dtype)

    def main_block(i, _):
        q_block(pl.multiple_of(i * block, block), block, i)
        return 0

    lax.fori_loop(0, n_main, main_block, 0)
    if tail:
        q_block(n_main * block, tail, n_main)


def da_attention(qkv, lam_q1, lam_k1, lam_q2, lam_k2, subln_g, *, lambda_init,
                 block=ATTN_BLOCK):
    b, seq_len, three_d = qkv.shape
    d = three_d // 3
    hw = 2 * DA_HEAD_DIM
    n_heads = d // hw
    slopes = jnp.asarray(
        np.array([2.0 ** (-8.0 * (i + 1) / n_heads) for i in range(n_heads)],
                 dtype=np.float32))
    vec = lambda a: a.reshape(1, -1).astype(F32)
    small = lambda n: pl.BlockSpec((1, n), lambda bi, hi: (0, 0))
    return pl.pallas_call(
        functools.partial(_da_attn_kernel, seq_len=seq_len, block=block,
                          lambda_init=lambda_init),
        out_shape=jax.ShapeDtypeStruct((b, seq_len, d), BF16),
        grid=(b, n_heads),
        in_specs=[
            pl.BlockSpec(memory_space=pltpu.SMEM),
            small(DA_HEAD_DIM), small(DA_HEAD_DIM), small(DA_HEAD_DIM),
            small(DA_HEAD_DIM),
            pl.BlockSpec((None, seq_len, hw), lambda bi, hi: (bi, 0, hi)),
            pl.BlockSpec((None, seq_len, hw), lambda bi, hi: (bi, 0, n_heads + hi)),
            pl.BlockSpec((None, seq_len, hw),
                         lambda bi, hi: (bi, 0, 2 * n_heads + hi)),
            small(hw),
        ],
        out_specs=pl.BlockSpec((None, seq_len, hw), lambda bi, hi: (bi, 0, hi)),
        compiler_params=pltpu.CompilerParams(
            dimension_semantics=("parallel", "parallel"),
            vmem_limit_bytes=VMEM_LIMIT_BYTES),
        name="da_attention",
    )(slopes, vec(lam_q1), vec(lam_k1), vec(lam_q2), vec(lam_k2), qkv, qkv, qkv,
      vec(subln_g))


def _sb_attn_kernel(q_ref, k_ref, v_ref, o_ref, *, seq_len, block):
    n_main = seq_len // block
    tail = seq_len - n_main * block

    def suffix_matrix(n):
        r = lax.broadcasted_iota(jnp.int32, (n, n), 0)
        c = lax.broadcasted_iota(jnp.int32, (n, n), 1)
        return jnp.where(r > c, 1.0, 0.0).astype(BF16)

    tri_block = suffix_matrix(block)

    def sb_step(q, carry, k0, nk, tri, diagonal):
        rest, acc = carry
        kb = k_ref[pl.ds(k0, nk), :]
        vb = v_ref[pl.ds(k0, nk), :]
        z = _dot_nt(q, kb)
        sp = jnp.log1p(jnp.exp(-jnp.abs(z)))
        log_beta = jnp.minimum(z, 0.0) - sp
        log_1m = log_beta - z
        if diagonal:
            nq = q.shape[0]
            row = lax.broadcasted_iota(jnp.int32, (nq, nk), 0)
            col = lax.broadcasted_iota(jnp.int32, (nq, nk), 1)
            strict = row > col
            log_1m = jnp.where(strict, log_1m, 0.0)
        hi = log_1m.astype(BF16)
        lo = (log_1m - hi.astype(F32)).astype(BF16)
        keep = (jnp.dot(hi, tri, preferred_element_type=F32)
                + jnp.dot(lo, tri, preferred_element_type=F32))
        attn = jnp.exp(log_beta + keep + rest)
        if diagonal:
            attn = jnp.where(strict, attn, 0.0)
        acc = acc + jnp.dot(attn.astype(BF16), vb, preferred_element_type=F32)
        rest = rest + jnp.sum(log_1m, axis=-1, keepdims=True)
        return rest, acc

    def q_block(q0, nq, n_full, tri_diag):
        q = q_ref[pl.ds(q0, nq), :]
        init = (jnp.zeros((nq, 1), F32), jnp.zeros((nq, SB_HEAD_DIM), F32))
        carry = sb_step(q, init, q0, nq, tri_diag, True)

        def full_step(t, carry):
            k0 = pl.multiple_of((n_full - 1 - t) * block, block)
            return sb_step(q, carry, k0, block, tri_block, False)

        _, acc = lax.fori_loop(0, n_full, full_step, carry)
        o_ref[pl.ds(q0, nq), :] = acc.astype(o_ref.dtype)

    def main_block(i, _):
        q_block(pl.multiple_of(i * block, block), block, i, tri_block)
        return 0

    lax.fori_loop(0, n_main, main_block, 0)
    if tail:
        q_block(n_main * block, tail, n_main, suffix_matrix(tail))


def sb_attention(q, kv, *, block=ATTN_BLOCK):
    b, seq_len, d = q.shape
    hw = SB_HEAD_DIM
    n_heads = d // hw
    return pl.pallas_call(
        functools.partial(_sb_attn_kernel, seq_len=seq_len, block=block),
        out_shape=jax.ShapeDtypeStruct((b, seq_len, d), BF16),
        grid=(b, n_heads),
        in_specs=[
            pl.BlockSpec((None, seq_len, hw), lambda bi, hi: (bi, 0, hi)),
            pl.BlockSpec((None, seq_len, hw), lambda bi, hi: (bi, 0, hi)),
            pl.BlockSpec((None, seq_len, hw), lambda bi, hi: (bi, 0, n_heads + hi)),
        ],
        out_specs=pl.BlockSpec((None, seq_len, hw), lambda bi, hi: (bi, 0, hi)),
        compiler_params=pltpu.CompilerParams(
            dimension_semantics=("parallel", "parallel"),
            vmem_limit_bytes=VMEM_LIMIT_BYTES),
        name="sb_attention",
    )(q, kv, kv)


def kernel(x, meta_tokens, attn_norm_g, mlp_norm_g, da_w_qkv, da_w_o, da_lambda_q1,
           da_lambda_k1, da_lambda_q2, da_lambda_k2, da_subln_g, kv_norm_g, sb_w_k,
           sb_w_v, sb_w_q, sb_w_o, mlp_w_up, mlp_w_down, final_norm_g):
    b, seq, d = x.shape
    n_meta = meta_tokens.shape[0]
    depth = attn_norm_g.shape[0]
    n_a = da_w_qkv.shape[0]
    seq_len = n_meta + seq

    meta = jnp.broadcast_to(meta_tokens.astype(x.dtype)[None], (b, n_meta, d))
    h = jnp.concatenate([meta, x], axis=1).reshape(b * seq_len, d)

    kv = None
    for i in range(depth):
        if i < n_a:
            lambda_init = 0.8 - 0.6 * math.exp(-0.3 * i)
            qkv = norm_matmul(h, attn_norm_g[i], da_w_qkv[i].astype(BF16))
            o = da_attention(qkv.reshape(b, seq_len, 3 * d), da_lambda_q1[i],
                             da_lambda_k1[i], da_lambda_q2[i], da_lambda_k2[i],
                             da_subln_g[i], lambda_init=lambda_init)
            w_o = da_w_o[i]
        else:
            j = i - n_a
            if kv is None:
                w_kv = jnp.concatenate([sb_w_k, sb_w_v], axis=1).astype(BF16)
                kv = norm_matmul(h, kv_norm_g, w_kv).reshape(b, seq_len, 2 * d)
            q = norm_matmul(h, attn_norm_g[i], sb_w_q[j].astype(BF16),
                            out_scale=SB_HEAD_DIM ** -0.5)
            o = sb_attention(q.reshape(b, seq_len, d), kv)
            w_o = sb_w_o[j]
        h = matmul_residual(h, o.reshape(b * seq_len, d), w_o.astype(BF16))
        h = mlp(h, mlp_norm_g[i], mlp_w_up[i].astype(BF16),
                mlp_w_down[i].astype(BF16))
    out = final_norm(h, final_norm_g)
    return out.reshape(b, seq_len, d)[:, n_meta:]
```

```python
import functools
import math

import jax
import jax.numpy as jnp
import numpy as np
from jax import lax
from jax.experimental import pallas as pl
from jax.experimental.pallas import tpu as pltpu

F32 = jnp.float32
BF16 = jnp.bfloat16

RMS_EPS = 1e-6
LOG2_E = math.log2(math.e)
DA_HEAD_DIM = 64
SB_HEAD_DIM = 128
ATTN_BLOCK = 256
SUBLANES = 8
BF16_SUBLANES = 16
LANES = 128
VMEM_LIMIT_BYTES = 56 * 1024 * 1024


def _pick_tile(n, target, multiple=BF16_SUBLANES):
    best = None
    for t in range(multiple, min(n, target) + 1, multiple):
        if n % t == 0:
            best = t
    if best is None:
        raise ValueError(f"no tile for {n=} {target=}")
    return best


def _rmsnorm_rows(x, g):
    ms = jnp.mean(x * x, axis=-1, keepdims=True)
    return x * lax.rsqrt(ms + RMS_EPS) * g


def _norm_matmul_kernel(h_ref, g_ref, w_ref, o_ref, xn_ref, *, out_scale):
    @pl.when(pl.program_id(1) == 0)
    def _():
        xn_ref[...] = _rmsnorm_rows(h_ref[...], g_ref[...]).astype(BF16)

    acc = jnp.dot(xn_ref[...], w_ref[...], preferred_element_type=F32)
    if out_scale != 1.0:
        acc = acc * out_scale
    o_ref[...] = acc.astype(o_ref.dtype)


def norm_matmul(h, g, w, *, out_scale=1.0, tm_target=768, tn_target=1024):
    m, d = h.shape
    n = w.shape[1]
    tm = _pick_tile(m, tm_target)
    tn = _pick_tile(n, tn_target, LANES)
    return pl.pallas_call(
        functools.partial(_norm_matmul_kernel, out_scale=out_scale),
        out_shape=jax.ShapeDtypeStruct((m, n), BF16),
        grid=(m // tm, n // tn),
        in_specs=[
            pl.BlockSpec((tm, d), lambda i, j: (i, 0)),
            pl.BlockSpec((1, d), lambda i, j: (0, 0)),
            pl.BlockSpec((d, tn), lambda i, j: (0, j)),
        ],
        out_specs=pl.BlockSpec((tm, tn), lambda i, j: (i, j)),
        scratch_shapes=[pltpu.VMEM((tm, d), BF16)],
        compiler_params=pltpu.CompilerParams(
            dimension_semantics=("parallel", "arbitrary"),
            vmem_limit_bytes=VMEM_LIMIT_BYTES),
        name="norm_matmul",
    )(h, g.reshape(1, d), w)


def _matmul_residual_kernel(h_ref, o_ref, w_ref, out_ref):
    out_ref[...] = h_ref[...] + jnp.dot(o_ref[...], w_ref[...],
                                        preferred_element_type=F32)


def matmul_residual(h, o, w, *, tm_target=768, tn_target=1024):
    m, d = h.shape
    k = o.shape[1]
    tm = _pick_tile(m, tm_target)
    tn = _pick_tile(d, tn_target, LANES)
    return pl.pallas_call(
        _matmul_residual_kernel,
        out_shape=jax.ShapeDtypeStruct((m, d), F32),
        grid=(m // tm, d // tn),
        in_specs=[
            pl.BlockSpec((tm, tn), lambda i, j: (i, j)),
            pl.BlockSpec((tm, k), lambda i, j: (i, 0)),
            pl.BlockSpec((k, tn), lambda i, j: (0, j)),
        ],
        out_specs=pl.BlockSpec((tm, tn), lambda i, j: (i, j)),
        compiler_params=pltpu.CompilerParams(
            dimension_semantics=("parallel", "arbitrary"),
            vmem_limit_bytes=VMEM_LIMIT_BYTES),
        name="matmul_residual",
    )(h, o, w)


def _mlp_kernel(h_ref, g_ref, wu_ref, wd_ref, out_ref, xn_ref):
    @pl.when(pl.program_id(1) == 0)
    def _():
        x = h_ref[...]
        xn_ref[...] = _rmsnorm_rows(x, g_ref[...]).astype(BF16)
        out_ref[...] = x

    a = jnp.dot(xn_ref[...], wu_ref[...], preferred_element_type=F32)
    a = jnp.maximum(a, 0.0)
    a = (a * a).astype(BF16)
    out_ref[...] += jnp.dot(a, wd_ref[...], preferred_element_type=F32)


def mlp(h, g, w_up, w_down, *, tm_target=768, tf_target=512):
    m, d = h.shape
    f = w_up.shape[1]
    tm = _pick_tile(m, tm_target)
    tf = _pick_tile(f, tf_target, LANES)
    return pl.pallas_call(
        _mlp_kernel,
        out_shape=jax.ShapeDtypeStruct((m, d), F32),
        grid=(m // tm, f // tf),
        in_specs=[
            pl.BlockSpec((tm, d), lambda i, j: (i, 0)),
            pl.BlockSpec((1, d), lambda i, j: (0, 0)),
            pl.BlockSpec((d, tf), lambda i, j: (0, j)),
            pl.BlockSpec((tf, d), lambda i, j: (j, 0)),
        ],
        out_specs=pl.BlockSpec((tm, d), lambda i, j: (i, 0)),
        scratch_shapes=[pltpu.VMEM((tm, d), BF16)],
        compiler_params=pltpu.CompilerParams(
            dimension_semantics=("parallel", "arbitrary"),
            vmem_limit_bytes=VMEM_LIMIT_BYTES),
        name="mlp",
    )(h, g.reshape(1, d), w_up, w_down)


def _final_norm_kernel(h_ref, g_ref, o_ref):
    o_ref[...] = _rmsnorm_rows(h_ref[...], g_ref[...])


def final_norm(h, g, *, tm_target=768):
    m, d = h.shape
    tm = _pick_tile(m, tm_target)
    return pl.pallas_call(
        _final_norm_kernel,
        out_shape=jax.ShapeDtypeStruct((m, d), F32),
        grid=(m // tm,),
        in_specs=[
            pl.BlockSpec((tm, d), lambda i: (i, 0)),
            pl.BlockSpec((1, d), lambda i: (0, 0)),
        ],
        out_specs=pl.BlockSpec((tm, d), lambda i: (i, 0)),
        compiler_params=pltpu.CompilerParams(
            dimension_semantics=("parallel",),
            vmem_limit_bytes=VMEM_LIMIT_BYTES),
        name="final_norm",
    )(h, g.reshape(1, d))


def _dot_tn(a, b):
    return lax.dot_general(a, b, (((0,), (0,)), ((), ())),
                           preferred_element_type=F32)


def _causal_schedule(n_blocks, reverse):
    steps = []
    for qi in range(n_blocks):
        ks = range(qi, -1, -1) if reverse else range(qi + 1)
        steps.extend((qi, kj) for kj in ks)
    steps = [(n_blocks, 0)] + steps + [steps[-1]]
    qi_tab = np.array([s[0] for s in steps], np.int32)
    kj_tab = np.array([s[1] for s in steps], np.int32)
    return jnp.asarray(qi_tab), jnp.asarray(kj_tab)


def _da_attn_kernel(qi_tab, kj_tab, slopes_ref, lq1_ref, lk1_ref, lq2_ref, lk2_ref,
                    q_ref, k_ref, v_ref, g_ref, o_ref,
                    qt_ref, s_a, s_b, p_a, p_b, fin_ref, m_ref, l_ref, bias_ref,
                    *, seq_len, block, lambda_init):
    hd = DA_HEAD_DIM
    hw = 2 * hd
    nq = block
    width = 2 * nq
    n_blocks = seq_len // block
    head = seq_len - n_blocks * block
    n_steps = n_blocks * (n_blocks + 1) // 2
    slope = slopes_ref[pl.program_id(1)] * LOG2_E
    lam = (jnp.exp(jnp.sum(lq1_ref[...] * lk1_ref[...], axis=-1, keepdims=True))
           - jnp.exp(jnp.sum(lq2_ref[...] * lk2_ref[...], axis=-1, keepdims=True))
           + lambda_init)
    scale = hd ** -0.5 * LOG2_E
    out_gain = g_ref[...] * (1.0 - lambda_init)

    def to_f32(i):
        return jnp.asarray(i, jnp.int32).astype(F32)

    key = lax.broadcasted_iota(jnp.int32, (block, width), 0)
    qry = lax.broadcasted_iota(jnp.int32, (block, width), 1)
    qry = jnp.where(qry >= nq, qry - nq, qry)
    bias = slope * key.astype(F32)
    bias_ref[0] = bias
    bias_ref[1] = jnp.where(key <= qry, bias, -jnp.inf)

    first_map = lax.broadcasted_iota(jnp.int32, (hw, nq), 0) < hd

    def prepare(qi):
        q0 = head + qi * block
        qt = (q_ref[pl.ds(q0, nq), :].astype(F32) * scale).T
        qt = jnp.concatenate([jnp.where(first_map, qt, 0.0),
                              jnp.where(first_map, 0.0, qt)], axis=1).astype(BF16)
        qt_ref[qi] = qt
        if head:
            s = jnp.dot(k_ref[pl.ds(0, head), :], qt, preferred_element_type=F32)
            s = s + slope * lax.broadcasted_iota(jnp.int32, (head, width), 0).astype(F32)
            m_loc = jnp.max(s, axis=0, keepdims=True)
            p = jnp.exp2(s - m_loc)
            m_ref[qi] = m_loc - slope * to_f32(q0)
            l_ref[qi] = jnp.sum(p, axis=0, keepdims=True)
            fin_ref[qi] = _dot_tn(v_ref[pl.ds(0, head), :], p.astype(BF16))
        else:
            m_ref[qi] = jnp.full((1, width), -jnp.inf, F32)
            l_ref[qi] = jnp.zeros((1, width), F32)
            fin_ref[qi] = jnp.zeros((hw, width), F32)

    for qi in range(n_blocks):
        prepare(qi)
    fin_ref[n_blocks] = jnp.zeros((hw, width), F32)

    def key_rows(kj):
        return pl.ds(pl.multiple_of(head + kj * block, BF16_SUBLANES), block)

    def scores(t, s_out):
        s_out[...] = jnp.dot(k_ref[key_rows(kj_tab[t + 1]), :], qt_ref[qi_tab[t + 1]],
                             preferred_element_type=F32)

    def add_values(t, p_in):
        qi = qi_tab[t + 1]
        fin_ref[qi] = fin_ref[qi] + _dot_tn(v_ref[key_rows(kj_tab[t + 1]), :], p_in[...])

    def step(t, s_cur, s_next, p_prev, p_cur):
        scores(t + 1, s_next)
        add_values(t - 1, p_prev)
        qi = qi_tab[t + 1]
        kj = kj_tab[t + 1]
        s = s_cur[...] + bias_ref[(qi == kj).astype(jnp.int32)]
        c = slope * to_f32((kj - qi) * block)
        m_old = m_ref[qi]
        m_new = jnp.maximum(m_old, jnp.max(s, axis=0, keepdims=True) + c)
        alpha = jnp.exp2(m_old - m_new)
        p = jnp.exp2(s - (m_new - c))
        p_cur[...] = p.astype(BF16)
        l_ref[qi] = alpha * l_ref[qi] + jnp.sum(p, axis=0, keepdims=True)
        m_ref[qi] = m_new
        fin_ref[qi] = alpha * fin_ref[qi]

    scores(0, s_a)
    p_b[...] = jnp.zeros((block, width), BF16)

    def double_step(u, _):
        step(2 * u, s_a, s_b, p_b, p_a)
        step(2 * u + 1, s_b, s_a, p_a, p_b)
        return 0

    lax.fori_loop(0, n_steps // 2, double_step, 0)
    add_values(n_steps - 1, p_b)

    def finish_rows(acc, l, nrows):
        o = acc / l
        o = o[:, :nrows] - lam * o[:, nrows:]
        ms = jnp.mean(o * o, axis=0, keepdims=True)
        return (o * lax.rsqrt(ms + RMS_EPS)).T * out_gain

    for qi in range(n_blocks):
        o = finish_rows(fin_ref[qi], l_ref[qi], nq)
        o_ref[pl.ds(head + qi * block, nq), :] = o.astype(o_ref.dtype)

    if head:
        n_pad = LANES // 2
        q = q_ref[pl.ds(0, head), :].astype(F32) * scale
        q = jnp.concatenate([q, jnp.zeros((n_pad - head, hw), F32)], axis=0)
        first = lax.broadcasted_iota(jnp.int32, (hw, n_pad), 0) < hd
        qt = q.T
        qt = jnp.concatenate([jnp.where(first, qt, 0.0), jnp.where(first, 0.0, qt)],
                             axis=1).astype(BF16)
        s = jnp.dot(k_ref[pl.ds(0, head), :], qt, preferred_element_type=F32)
        key_h = lax.broadcasted_iota(jnp.int32, (head, 2 * n_pad), 0)
        qry_h = lax.broadcasted_iota(jnp.int32, (head, 2 * n_pad), 1)
        qry_h = jnp.where(qry_h >= n_pad, qry_h - n_pad, qry_h)
        s = jnp.where(key_h <= qry_h, s + slope * key_h.astype(F32), -jnp.inf)
        p = jnp.exp2(s - jnp.max(s, axis=0, keepdims=True))
        l = jnp.sum(p, axis=0, keepdims=True)
        acc = _dot_tn(v_ref[pl.ds(0, head), :], p.astype(BF16))
        o = finish_rows(acc, l, n_pad)
        o_ref[pl.ds(0, head), :] = o[:head].astype(o_ref.dtype)


def da_attention(qkv, lam_q1, lam_k1, lam_q2, lam_k2, subln_g, *, lambda_init,
                 block=ATTN_BLOCK):
    b, seq_len, three_d = qkv.shape
    d = three_d // 3
    hw = 2 * DA_HEAD_DIM
    n_heads = d // hw
    n_blocks = seq_len // block
    n_steps = n_blocks * (n_blocks + 1) // 2
    head = seq_len - n_blocks * block
    assert n_steps % 2 == 0 and head <= LANES // 2 and head % BF16_SUBLANES == 0
    qi_tab, kj_tab = _causal_schedule(n_blocks, reverse=False)
    slopes = jnp.asarray(
        np.array([2.0 ** (-8.0 * (i + 1) / n_heads) for i in range(n_heads)],
                 dtype=np.float32))
    vec = lambda a: a.reshape(1, -1).astype(F32)
    small = lambda n: pl.BlockSpec((1, n), lambda bi, hi: (0, 0))
    smem = pl.BlockSpec(memory_space=pltpu.SMEM)
    width = 2 * block
    return pl.pallas_call(
        functools.partial(_da_attn_kernel, seq_len=seq_len, block=block,
                          lambda_init=lambda_init),
        out_shape=jax.ShapeDtypeStruct((b, seq_len, d), BF16),
        grid=(b, n_heads),
        in_specs=[
            smem, smem, smem,
            small(DA_HEAD_DIM), small(DA_HEAD_DIM), small(DA_HEAD_DIM),
            small(DA_HEAD_DIM),
            pl.BlockSpec((None, seq_len, hw), lambda bi, hi: (bi, 0, hi)),
            pl.BlockSpec((None, seq_len, hw), lambda bi, hi: (bi, 0, n_heads + hi)),
            pl.BlockSpec((None, seq_len, hw),
                         lambda bi, hi: (bi, 0, 2 * n_heads + hi)),
            small(hw),
        ],
        out_specs=pl.BlockSpec((None, seq_len, hw), lambda bi, hi: (bi, 0, hi)),
        scratch_shapes=[
            pltpu.VMEM((n_blocks, hw, width), BF16),
            pltpu.VMEM((block, width), F32),
            pltpu.VMEM((block, width), F32),
            pltpu.VMEM((block, width), BF16),
            pltpu.VMEM((block, width), BF16),
            pltpu.VMEM((n_blocks + 1, hw, width), F32),
            pltpu.VMEM((n_blocks, 1, width), F32),
            pltpu.VMEM((n_blocks, 1, width), F32),
            pltpu.VMEM((2, block, width), F32),
        ],
        compiler_params=pltpu.CompilerParams(
            dimension_semantics=("parallel", "parallel"),
            vmem_limit_bytes=VMEM_LIMIT_BYTES),
        name="da_attention",
    )(qi_tab, kj_tab, slopes, vec(lam_q1), vec(lam_k1), vec(lam_q2), vec(lam_k2),
      qkv, qkv, qkv, vec(subln_g))


MASKED = 1e30


def interleave_key_rows(x, head, block):
    b, seq_len, c = x.shape
    n_blocks = (seq_len - head) // block
    body = x[:, head:].reshape(b, n_blocks, SUBLANES, block // SUBLANES, c)
    body = jnp.swapaxes(body, 2, 3).reshape(b, n_blocks * block, c)
    return x[:, :head], body


def _log2_sigmoid_pair(z2):
    sign = jnp.int32(-2 ** 31)
    neg_abs = lax.bitcast_convert_type(lax.bitcast_convert_type(z2, jnp.int32) | sign, F32)
    sp = jnp.log2(1.0 + jnp.exp2(neg_abs))
    log_beta = jnp.minimum(z2, 0.0) - sp
    return log_beta, log_beta - z2


def _sb_attn_kernel(qi_tab, kj_tab, q_ref, kh_ref, vh_ref, k_ref, v_ref, o_ref,
                    qt_ref, s_a, s_b, p_a, p_b, fin_ref, rest_ref, mask_ref,
                    *, seq_len, block):
    hw = SB_HEAD_DIM
    nq = block
    n_blocks = seq_len // block
    head = seq_len - n_blocks * block
    n_steps = n_blocks * (n_blocks + 1) // 2
    groups = block // SUBLANES

    row = lax.broadcasted_iota(jnp.int32, (block, nq), 0)
    qry = lax.broadcasted_iota(jnp.int32, (block, nq), 1)
    key = (row >> int(math.log2(SUBLANES))) + groups * (row & (SUBLANES - 1))
    mask_ref[0] = jnp.zeros((block, nq), F32)
    mask_ref[1] = jnp.where(key < qry, 0.0, -MASKED)

    for qi in range(n_blocks):
        q0 = head + qi * block
        qt_ref[qi] = q_ref[pl.ds(q0, nq), :].astype(F32).T.astype(BF16)
        rest_ref[qi] = jnp.zeros((1, nq), F32)
        fin_ref[qi] = jnp.zeros((hw, nq), F32)
    fin_ref[n_blocks] = jnp.zeros((hw, nq), F32)

    def key_rows(kj):
        return pl.ds(pl.multiple_of(kj * block, block), block)

    def scores(t, s_out):
        s_out[...] = jnp.dot(k_ref[key_rows(kj_tab[t + 1]), :], qt_ref[qi_tab[t + 1]],
                             preferred_element_type=F32)

    def add_values(t, p_in):
        qi = qi_tab[t + 1]
        fin_ref[qi] = fin_ref[qi] + _dot_tn(v_ref[key_rows(kj_tab[t + 1]), :], p_in[...])

    sub = lax.broadcasted_iota(jnp.int32, (SUBLANES, nq), 0)

    def step(t, s_cur, s_next, p_prev, p_cur):
        scores(t + 1, s_next)
        add_values(t - 1, p_prev)
        qi = qi_tab[t + 1]
        kj = kj_tab[t + 1]
        z = s_cur[...] + mask_ref[(qi == kj).astype(jnp.int32)]
        log_beta, log_1m = _log2_sigmoid_pair(z)
        later = [None] * groups
        run = jnp.zeros((SUBLANES, nq), F32)
        for r in reversed(range(groups)):
            later[r] = run
            run = run + log_1m[r * SUBLANES:(r + 1) * SUBLANES, :]
        above = jnp.zeros((SUBLANES, nq), F32)
        for d in range(1, SUBLANES):
            shifted = pltpu.roll(run, SUBLANES - d, 0)
            above = above + jnp.where(sub < SUBLANES - d, shifted, 0.0)
        rest = rest_ref[qi]
        base = above + rest
        keep = jnp.concatenate([g + base for g in later], axis=0)
        p_cur[...] = jnp.exp2(log_beta + keep).astype(BF16)
        rest_ref[qi] = rest + (above + run)[0:1, :]

    scores(0, s_a)
    p_b[...] = jnp.zeros((block, nq), BF16)

    def double_step(u, _):
        step(2 * u, s_a, s_b, p_b, p_a)
        step(2 * u + 1, s_b, s_a, p_a, p_b)
        return 0

    lax.fori_loop(0, n_steps // 2, double_step, 0)
    add_values(n_steps - 1, p_b)

    def suffix_matrix(n):
        r = lax.broadcasted_iota(jnp.int32, (n, n), 0)
        c = lax.broadcasted_iota(jnp.int32, (n, n), 1)
        return jnp.where(c > r, 1.0, 0.0).astype(BF16)

    def later_sum(x, tri):
        hi = x.astype(BF16)
        lo = (x - hi.astype(F32)).astype(BF16)
        return (jnp.dot(tri, hi, preferred_element_type=F32)
                + jnp.dot(tri, lo, preferred_element_type=F32))

    tri = suffix_matrix(head)
    kh = kh_ref[...]
    vh = vh_ref[...]

    for qi in range(n_blocks):
        z = jnp.dot(kh, qt_ref[qi], preferred_element_type=F32)
        log_beta, log_1m = _log2_sigmoid_pair(z)
        keep = later_sum(log_1m, tri) + rest_ref[qi]
        attn = jnp.exp2(log_beta + keep)
        acc = fin_ref[qi] + _dot_tn(vh, attn.astype(BF16))
        o_ref[pl.ds(head + qi * block, nq), :] = acc.T.astype(o_ref.dtype)

    qt = q_ref[pl.ds(0, head), :].astype(F32)
    qt = jnp.concatenate([qt, jnp.zeros((LANES - head, hw), F32)], axis=0)
    qt = qt.T.astype(BF16)
    z = jnp.dot(kh, qt, preferred_element_type=F32)
    key_h = lax.broadcasted_iota(jnp.int32, (head, LANES), 0)
    qry_h = lax.broadcasted_iota(jnp.int32, (head, LANES), 1)
    z = jnp.where(key_h < qry_h, z, -MASKED)
    log_beta, log_1m = _log2_sigmoid_pair(z)
    attn = jnp.exp2(log_beta + later_sum(log_1m, tri))
    acc = _dot_tn(vh, attn.astype(BF16))
    o_ref[pl.ds(0, head), :] = acc.T[:head].astype(o_ref.dtype)


def sb_attention(q, kv_head, kv_body, *, block=ATTN_BLOCK):
    b, seq_len, d = q.shape
    hw = SB_HEAD_DIM
    n_heads = d // hw
    n_blocks = seq_len // block
    head = seq_len - n_blocks * block
    n_steps = n_blocks * (n_blocks + 1) // 2
    assert n_steps % 2 == 0 and 0 < head <= LANES and head % BF16_SUBLANES == 0
    qi_tab, kj_tab = _causal_schedule(n_blocks, reverse=True)
    smem = pl.BlockSpec(memory_space=pltpu.SMEM)
    body_len = n_blocks * block
    return pl.pallas_call(
        functools.partial(_sb_attn_kernel, seq_len=seq_len, block=block),
        out_shape=jax.ShapeDtypeStruct((b, seq_len, d), BF16),
        grid=(b, n_heads),
        in_specs=[
            smem, smem,
            pl.BlockSpec((None, seq_len, hw), lambda bi, hi: (bi, 0, hi)),
            pl.BlockSpec((None, head, hw), lambda bi, hi: (bi, 0, hi)),
            pl.BlockSpec((None, head, hw), lambda bi, hi: (bi, 0, n_heads + hi)),
            pl.BlockSpec((None, body_len, hw), lambda bi, hi: (bi, 0, hi)),
            pl.BlockSpec((None, body_len, hw), lambda bi, hi: (bi, 0, n_heads + hi)),
        ],
        out_specs=pl.BlockSpec((None, seq_len, hw), lambda bi, hi: (bi, 0, hi)),
        scratch_shapes=[
            pltpu.VMEM((n_blocks, hw, block), BF16),
            pltpu.VMEM((block, block), F32),
            pltpu.VMEM((block, block), F32),
            pltpu.VMEM((block, block), BF16),
            pltpu.VMEM((block, block), BF16),
            pltpu.VMEM((n_blocks + 1, hw, block), F32),
            pltpu.VMEM((n_blocks, 1, block), F32),
            pltpu.VMEM((2, block, block), F32),
        ],
        compiler_params=pltpu.CompilerParams(
            dimension_semantics=("parallel", "parallel"),
            vmem_limit_bytes=VMEM_LIMIT_BYTES),
        name="sb_attention",
    )(qi_tab, kj_tab, q, kv_head, kv_head, kv_body, kv_body)


def kernel(x, meta_tokens, attn_norm_g, mlp_norm_g, da_w_qkv, da_w_o, da_lambda_q1,
           da_lambda_k1, da_lambda_q2, da_lambda_k2, da_subln_g, kv_norm_g, sb_w_k,
           sb_w_v, sb_w_q, sb_w_o, mlp_w_up, mlp_w_down, final_norm_g):
    b, seq, d = x.shape
    n_meta = meta_tokens.shape[0]
    depth = attn_norm_g.shape[0]
    n_a = da_w_qkv.shape[0]
    seq_len = n_meta + seq

    meta = jnp.broadcast_to(meta_tokens.astype(x.dtype)[None], (b, n_meta, d))
    h = jnp.concatenate([meta, x], axis=1).reshape(b * seq_len, d)

    kv = None
    for i in range(depth):
        if i < n_a:
            lambda_init = 0.8 - 0.6 * math.exp(-0.3 * i)
            qkv = norm_matmul(h, attn_norm_g[i], da_w_qkv[i].astype(BF16))
            o = da_attention(qkv.reshape(b, seq_len, 3 * d), da_lambda_q1[i],
                             da_lambda_k1[i], da_lambda_q2[i], da_lambda_k2[i],
                             da_subln_g[i], lambda_init=lambda_init)
            w_o = da_w_o[i]
        else:
            j = i - n_a
            if kv is None:
                w_kv = jnp.concatenate([sb_w_k, sb_w_v], axis=1).astype(BF16)
                kv = norm_matmul(h, kv_norm_g, w_kv).reshape(b, seq_len, 2 * d)
                kv_head, kv_body = interleave_key_rows(kv, seq_len % ATTN_BLOCK,
                                                       ATTN_BLOCK)
            q = norm_matmul(h, attn_norm_g[i], sb_w_q[j].astype(BF16),
                            out_scale=SB_HEAD_DIM ** -0.5 * LOG2_E)
            o = sb_attention(q.reshape(b, seq_len, d), kv_head, kv_body)
            w_o = sb_w_o[j]
        h = matmul_residual(h, o.reshape(b * seq_len, d), w_o.astype(BF16))
        h = mlp(h, mlp_norm_g[i], mlp_w_up[i].astype(BF16),
                mlp_w_down[i].astype(BF16))
    out = final_norm(h, final_norm_g)
    return out.reshape(b, seq_len, d)[:, n_meta:]
```

```python
import functools
import math

import jax
import jax.numpy as jnp
import numpy as np
from jax import lax
from jax.experimental import pallas as pl
from jax.experimental.pallas import tpu as pltpu

F32 = jnp.float32
BF16 = jnp.bfloat16

RMS_EPS = 1e-6
LOG2_E = math.log2(math.e)
DA_HEAD_DIM = 64
SB_HEAD_DIM = 128
ATTN_BLOCK = 256
STEP_UNROLL = 4
SUBLANES = 8
BF16_SUBLANES = 16
LANES = 128
VMEM_LIMIT_BYTES = 56 * 1024 * 1024


def _pick_tile(n, target, multiple=BF16_SUBLANES):
    best = None
    for t in range(multiple, min(n, target) + 1, multiple):
        if n % t == 0:
            best = t
    if best is None:
        raise ValueError(f"no tile for {n=} {target=}")
    return best


def _rmsnorm_rows(x, g):
    ms = jnp.mean(x * x, axis=-1, keepdims=True)
    return x * lax.rsqrt(ms + RMS_EPS) * g


def _norm_matmul_kernel(h_ref, g_ref, w_ref, o_ref, xn_ref, *, out_scale):
    @pl.when(pl.program_id(1) == 0)
    def _():
        xn_ref[...] = _rmsnorm_rows(h_ref[...], g_ref[...]).astype(BF16)

    acc = jnp.dot(xn_ref[...], w_ref[...], preferred_element_type=F32)
    if out_scale != 1.0:
        acc = acc * out_scale
    o_ref[...] = acc.astype(o_ref.dtype)


def norm_matmul(h, g, w, *, out_scale=1.0, tm_target=768, tn_target=1024):
    m, d = h.shape
    n = w.shape[1]
    tm = _pick_tile(m, tm_target)
    tn = _pick_tile(n, tn_target, LANES)
    return pl.pallas_call(
        functools.partial(_norm_matmul_kernel, out_scale=out_scale),
        out_shape=jax.ShapeDtypeStruct((m, n), BF16),
        grid=(m // tm, n // tn),
        in_specs=[
            pl.BlockSpec((tm, d), lambda i, j: (i, 0)),
            pl.BlockSpec((1, d), lambda i, j: (0, 0)),
            pl.BlockSpec((d, tn), lambda i, j: (0, j)),
        ],
        out_specs=pl.BlockSpec((tm, tn), lambda i, j: (i, j)),
        scratch_shapes=[pltpu.VMEM((tm, d), BF16)],
        compiler_params=pltpu.CompilerParams(
            dimension_semantics=("parallel", "arbitrary"),
            vmem_limit_bytes=VMEM_LIMIT_BYTES),
        name="norm_matmul",
    )(h, g.reshape(1, d), w)


def _attn_out_mlp_kernel(h_ref, o_ref, wo_ref, g_ref, wu_ref, wd_ref, out_ref, xn_ref):
    @pl.when(pl.program_id(1) == 0)
    def _():
        x = h_ref[...] + jnp.dot(o_ref[...], wo_ref[...], preferred_element_type=F32)
        xn_ref[...] = _rmsnorm_rows(x, g_ref[...]).astype(BF16)
        out_ref[...] = x

    a = jnp.dot(xn_ref[...], wu_ref[...], preferred_element_type=F32)
    a = jnp.maximum(a, 0.0)
    a = (a * a).astype(BF16)
    out_ref[...] += jnp.dot(a, wd_ref[...], preferred_element_type=F32)


def attn_out_mlp(h, o, w_o, g, w_up, w_down, *, tm_target=768, tf_target=512):
    m, d = h.shape
    f = w_up.shape[1]
    tm = _pick_tile(m, tm_target)
    tf = _pick_tile(f, tf_target, LANES)
    return pl.pallas_call(
        _attn_out_mlp_kernel,
        out_shape=jax.ShapeDtypeStruct((m, d), F32),
        grid=(m // tm, f // tf),
        in_specs=[
            pl.BlockSpec((tm, d), lambda i, j: (i, 0)),
            pl.BlockSpec((tm, d), lambda i, j: (i, 0)),
            pl.BlockSpec((d, d), lambda i, j: (0, 0), pipeline_mode=pl.Buffered(1)),
            pl.BlockSpec((1, d), lambda i, j: (0, 0)),
            pl.BlockSpec((d, tf), lambda i, j: (0, j)),
            pl.BlockSpec((tf, d), lambda i, j: (j, 0)),
        ],
        out_specs=pl.BlockSpec((tm, d), lambda i, j: (i, 0)),
        scratch_shapes=[pltpu.VMEM((tm, d), BF16)],
        compiler_params=pltpu.CompilerParams(
            dimension_semantics=("parallel", "arbitrary"),
            vmem_limit_bytes=VMEM_LIMIT_BYTES),
        name="attn_out_mlp",
    )(h, o, w_o, g.reshape(1, d), w_up, w_down)


def _final_norm_kernel(h_ref, g_ref, o_ref):
    o_ref[...] = _rmsnorm_rows(h_ref[...], g_ref[...])


def final_norm(h, g, skip, *, tr_target=512):
    b, seq_len, d = h.shape
    rows = seq_len - skip
    assert skip % SUBLANES == 0
    tr = _pick_tile(rows, tr_target, SUBLANES)
    return pl.pallas_call(
        _final_norm_kernel,
        out_shape=jax.ShapeDtypeStruct((b, rows, d), F32),
        grid=(b, rows // tr),
        in_specs=[
            pl.BlockSpec((pl.Element(1), pl.Element(tr), pl.Element(d)),
                         lambda bi, i: (bi, pl.multiple_of(skip + i * tr, SUBLANES), 0)),
            pl.BlockSpec((1, 1, d), lambda bi, i: (0, 0, 0)),
        ],
        out_specs=pl.BlockSpec((1, tr, d), lambda bi, i: (bi, i, 0)),
        compiler_params=pltpu.CompilerParams(
            dimension_semantics=("parallel", "parallel"),
            vmem_limit_bytes=VMEM_LIMIT_BYTES),
        name="final_norm",
    )(h, g.reshape(1, 1, d))


def _dot_tn(a, b):
    return lax.dot_general(a, b, (((0,), (0,)), ((), ())),
                           preferred_element_type=F32)


def _causal_schedule(n_blocks, reverse):
    steps = []
    for qi in range(n_blocks):
        ks = range(qi, -1, -1) if reverse else range(qi + 1)
        steps.extend((qi, kj) for kj in ks)
    steps = [(n_blocks, 0)] + steps + [steps[-1]]
    qi_tab = np.array([s[0] for s in steps], np.int32)
    kj_tab = np.array([s[1] for s in steps], np.int32)
    return jnp.asarray(qi_tab), jnp.asarray(kj_tab)


def _da_attn_kernel(qi_tab, kj_tab, slopes_ref, lq1_ref, lk1_ref, lq2_ref, lk2_ref,
                    q_ref, k_ref, v_ref, g_ref, o_ref,
                    qt_ref, s_a, s_b, p_a, p_b, fin_ref, m_ref, l_ref, bias_ref,
                    *, seq_len, block, lambda_init):
    hd = DA_HEAD_DIM
    hw = 2 * hd
    nq = block
    width = 2 * nq
    n_blocks = seq_len // block
    head = seq_len - n_blocks * block
    n_steps = n_blocks * (n_blocks + 1) // 2
    slope = slopes_ref[pl.program_id(1)] * LOG2_E
    lam = (jnp.exp(jnp.sum(lq1_ref[...] * lk1_ref[...], axis=-1, keepdims=True))
           - jnp.exp(jnp.sum(lq2_ref[...] * lk2_ref[...], axis=-1, keepdims=True))
           + lambda_init)
    scale = hd ** -0.5 * LOG2_E
    out_gain = g_ref[...] * (1.0 - lambda_init)

    def to_f32(i):
        return jnp.asarray(i, jnp.int32).astype(F32)

    key = lax.broadcasted_iota(jnp.int32, (block, width), 0)
    qry = lax.broadcasted_iota(jnp.int32, (block, width), 1)
    qry = jnp.where(qry >= nq, qry - nq, qry)
    bias = slope * key.astype(F32)
    bias_ref[0] = bias
    bias_ref[1] = jnp.where(key <= qry, bias, -jnp.inf)

    first_map = lax.broadcasted_iota(jnp.int32, (hw, nq), 0) < hd

    def prepare(qi):
        q0 = head + qi * block
        qt = (q_ref[pl.ds(q0, nq), :].astype(F32) * scale).T
        qt = jnp.concatenate([jnp.where(first_map, qt, 0.0),
                              jnp.where(first_map, 0.0, qt)], axis=1).astype(BF16)
        qt_ref[qi] = qt
        if head:
            s = jnp.dot(k_ref[pl.ds(0, head), :], qt, preferred_element_type=F32)
            s = s + slope * lax.broadcasted_iota(jnp.int32, (head, width), 0).astype(F32)
            m_loc = jnp.max(s, axis=0, keepdims=True)
            p = jnp.exp2(s - m_loc)
            m_ref[qi] = m_loc - slope * to_f32(q0)
            l_ref[qi] = jnp.sum(p, axis=0, keepdims=True)
            fin_ref[qi] = _dot_tn(v_ref[pl.ds(0, head), :], p.astype(BF16))
        else:
            m_ref[qi] = jnp.full((1, width), -jnp.inf, F32)
            l_ref[qi] = jnp.zeros((1, width), F32)
            fin_ref[qi] = jnp.zeros((hw, width), F32)

    for qi in range(n_blocks):
        prepare(qi)
    fin_ref[n_blocks] = jnp.zeros((hw, width), F32)

    def key_rows(kj):
        return pl.ds(pl.multiple_of(head + kj * block, BF16_SUBLANES), block)

    def scores(t, s_out):
        s_out[...] = jnp.dot(k_ref[key_rows(kj_tab[t + 1]), :], qt_ref[qi_tab[t + 1]],
                             preferred_element_type=F32)

    def add_values(t, p_in):
        qi = qi_tab[t + 1]
        fin_ref[qi] = fin_ref[qi] + _dot_tn(v_ref[key_rows(kj_tab[t + 1]), :], p_in[...])

    def step(t, s_cur, s_next, p_prev, p_cur):
        scores(t + 1, s_next)
        add_values(t - 1, p_prev)
        qi = qi_tab[t + 1]
        kj = kj_tab[t + 1]
        s = s_cur[...] + bias_ref[(qi == kj).astype(jnp.int32)]
        c = slope * to_f32((kj - qi) * block)
        m_old = m_ref[qi]
        m_new = jnp.maximum(m_old, jnp.max(s, axis=0, keepdims=True) + c)
        alpha = jnp.exp2(m_old - m_new)
        p = jnp.exp2(s - (m_new - c))
        p_cur[...] = p.astype(BF16)
        l_ref[qi] = alpha * l_ref[qi] + jnp.sum(p, axis=0, keepdims=True)
        m_ref[qi] = m_new
        fin_ref[qi] = alpha * fin_ref[qi]

    scores(0, s_a)
    p_b[...] = jnp.zeros((block, width), BF16)

    def quad_step(u, _):
        step(4 * u, s_a, s_b, p_b, p_a)
        step(4 * u + 1, s_b, s_a, p_a, p_b)
        step(4 * u + 2, s_a, s_b, p_b, p_a)
        step(4 * u + 3, s_b, s_a, p_a, p_b)
        return 0

    lax.fori_loop(0, n_steps // STEP_UNROLL, quad_step, 0)
    add_values(n_steps - 1, p_b)

    def finish_rows(acc, l, nrows):
        o = acc / l
        o = o[:, :nrows] - lam * o[:, nrows:]
        ms = jnp.mean(o * o, axis=0, keepdims=True)
        return (o * lax.rsqrt(ms + RMS_EPS)).T * out_gain

    for qi in range(n_blocks):
        o = finish_rows(fin_ref[qi], l_ref[qi], nq)
        o_ref[pl.ds(head + qi * block, nq), :] = o.astype(o_ref.dtype)

    if head:
        n_pad = LANES // 2
        q = q_ref[pl.ds(0, head), :].astype(F32) * scale
        q = jnp.concatenate([q, jnp.zeros((n_pad - head, hw), F32)], axis=0)
        first = lax.broadcasted_iota(jnp.int32, (hw, n_pad), 0) < hd
        qt = q.T
        qt = jnp.concatenate([jnp.where(first, qt, 0.0), jnp.where(first, 0.0, qt)],
                             axis=1).astype(BF16)
        s = jnp.dot(k_ref[pl.ds(0, head), :], qt, preferred_element_type=F32)
        key_h = lax.broadcasted_iota(jnp.int32, (head, 2 * n_pad), 0)
        qry_h = lax.broadcasted_iota(jnp.int32, (head, 2 * n_pad), 1)
        qry_h = jnp.where(qry_h >= n_pad, qry_h - n_pad, qry_h)
        s = jnp.where(key_h <= qry_h, s + slope * key_h.astype(F32), -jnp.inf)
        p = jnp.exp2(s - jnp.max(s, axis=0, keepdims=True))
        l = jnp.sum(p, axis=0, keepdims=True)
        acc = _dot_tn(v_ref[pl.ds(0, head), :], p.astype(BF16))
        o = finish_rows(acc, l, n_pad)
        o_ref[pl.ds(0, head), :] = o[:head].astype(o_ref.dtype)


def da_attention(qkv, lam_q1, lam_k1, lam_q2, lam_k2, subln_g, *, lambda_init,
                 block=ATTN_BLOCK):
    b, seq_len, three_d = qkv.shape
    d = three_d // 3
    hw = 2 * DA_HEAD_DIM
    n_heads = d // hw
    n_blocks = seq_len // block
    n_steps = n_blocks * (n_blocks + 1) // 2
    head = seq_len - n_blocks * block
    assert n_steps % STEP_UNROLL == 0 and head <= LANES // 2 and head % BF16_SUBLANES == 0
    qi_tab, kj_tab = _causal_schedule(n_blocks, reverse=False)
    slopes = jnp.asarray(
        np.array([2.0 ** (-8.0 * (i + 1) / n_heads) for i in range(n_heads)],
                 dtype=np.float32))
    vec = lambda a: a.reshape(1, -1).astype(F32)
    small = lambda n: pl.BlockSpec((1, n), lambda bi, hi: (0, 0))
    smem = pl.BlockSpec(memory_space=pltpu.SMEM)
    width = 2 * block
    return pl.pallas_call(
        functools.partial(_da_attn_kernel, seq_len=seq_len, block=block,
                          lambda_init=lambda_init),
        out_shape=jax.ShapeDtypeStruct((b, seq_len, d), BF16),
        grid=(b, n_heads),
        in_specs=[
            smem, smem, smem,
            small(DA_HEAD_DIM), small(DA_HEAD_DIM), small(DA_HEAD_DIM),
            small(DA_HEAD_DIM),
            pl.BlockSpec((None, seq_len, hw), lambda bi, hi: (bi, 0, hi)),
            pl.BlockSpec((None, seq_len, hw), lambda bi, hi: (bi, 0, n_heads + hi)),
            pl.BlockSpec((None, seq_len, hw),
                         lambda bi, hi: (bi, 0, 2 * n_heads + hi)),
            small(hw),
        ],
        out_specs=pl.BlockSpec((None, seq_len, hw), lambda bi, hi: (bi, 0, hi)),
        scratch_shapes=[
            pltpu.VMEM((n_blocks, hw, width), BF16),
            pltpu.VMEM((block, width), F32),
            pltpu.VMEM((block, width), F32),
            pltpu.VMEM((block, width), BF16),
            pltpu.VMEM((block, width), BF16),
            pltpu.VMEM((n_blocks + 1, hw, width), F32),
            pltpu.VMEM((n_blocks, 1, width), F32),
            pltpu.VMEM((n_blocks, 1, width), F32),
            pltpu.VMEM((2, block, width), F32),
        ],
        compiler_params=pltpu.CompilerParams(
            dimension_semantics=("parallel", "parallel"),
            vmem_limit_bytes=VMEM_LIMIT_BYTES),
        name="da_attention",
    )(qi_tab, kj_tab, slopes, vec(lam_q1), vec(lam_k1), vec(lam_q2), vec(lam_k2),
      qkv, qkv, qkv, vec(subln_g))


MASKED = 1e30


def interleave_key_rows(x, head, block):
    b, seq_len, c = x.shape
    n_blocks = (seq_len - head) // block
    body = x[:, head:].reshape(b, n_blocks, SUBLANES, block // SUBLANES, c)
    body = jnp.swapaxes(body, 2, 3).reshape(b, n_blocks * block, c)
    return x[:, :head], body


def _log2_sigmoid_pair(z2):
    sign = jnp.int32(-2 ** 31)
    neg_abs = lax.bitcast_convert_type(lax.bitcast_convert_type(z2, jnp.int32) | sign, F32)
    sp = jnp.log2(1.0 + jnp.exp2(neg_abs))
    log_beta = jnp.minimum(z2, 0.0) - sp
    return log_beta, log_beta - z2


def _sb_attn_kernel(qi_tab, kj_tab, q_ref, kh_ref, vh_ref, k_ref, v_ref, o_ref,
                    qt_ref, s_a, s_b, p_a, p_b, fin_ref, rest_ref, mask_ref,
                    *, seq_len, block):
    hw = SB_HEAD_DIM
    nq = block
    n_blocks = seq_len // block
    head = seq_len - n_blocks * block
    n_steps = n_blocks * (n_blocks + 1) // 2
    groups = block // SUBLANES

    row = lax.broadcasted_iota(jnp.int32, (block, nq), 0)
    qry = lax.broadcasted_iota(jnp.int32, (block, nq), 1)
    key = (row >> int(math.log2(SUBLANES))) + groups * (row & (SUBLANES - 1))
    mask_ref[0] = jnp.zeros((block, nq), F32)
    mask_ref[1] = jnp.where(key < qry, 0.0, -MASKED)

    for qi in range(n_blocks):
        q0 = head + qi * block
        qt_ref[qi] = q_ref[pl.ds(q0, nq), :].astype(F32).T.astype(BF16)
        rest_ref[qi] = jnp.zeros((1, nq), F32)
        fin_ref[qi] = jnp.zeros((hw, nq), F32)
    fin_ref[n_blocks] = jnp.zeros((hw, nq), F32)

    def key_rows(kj):
        return pl.ds(pl.multiple_of(kj * block, block), block)

    def scores(t, s_out):
        s_out[...] = jnp.dot(k_ref[key_rows(kj_tab[t + 1]), :], qt_ref[qi_tab[t + 1]],
                             preferred_element_type=F32)

    def add_values(t, p_in):
        qi = qi_tab[t + 1]
        fin_ref[qi] = fin_ref[qi] + _dot_tn(v_ref[key_rows(kj_tab[t + 1]), :], p_in[...])

    sub = lax.broadcasted_iota(jnp.int32, (SUBLANES, nq), 0)

    def step(t, s_cur, s_next, p_prev, p_cur):
        scores(t + 1, s_next)
        add_values(t - 1, p_prev)
        qi = qi_tab[t + 1]
        kj = kj_tab[t + 1]
        z = s_cur[...] + mask_ref[(qi == kj).astype(jnp.int32)]
        log_beta, log_1m = _log2_sigmoid_pair(z)
        later = [None] * groups
        run = jnp.zeros((SUBLANES, nq), F32)
        for r in reversed(range(groups)):
            later[r] = run
            run = run + log_1m[r * SUBLANES:(r + 1) * SUBLANES, :]
        above = jnp.zeros((SUBLANES, nq), F32)
        for d in range(1, SUBLANES):
            shifted = pltpu.roll(run, SUBLANES - d, 0)
            above = above + jnp.where(sub < SUBLANES - d, shifted, 0.0)
        rest = rest_ref[qi]
        base = above + rest
        keep = jnp.concatenate([g + base for g in later], axis=0)
        p_cur[...] = jnp.exp2(log_beta + keep).astype(BF16)
        rest_ref[qi] = rest + (above + run)[0:1, :]

    scores(0, s_a)
    p_b[...] = jnp.zeros((block, nq), BF16)

    def quad_step(u, _):
        step(4 * u, s_a, s_b, p_b, p_a)
        step(4 * u + 1, s_b, s_a, p_a, p_b)
        step(4 * u + 2, s_a, s_b, p_b, p_a)
        step(4 * u + 3, s_b, s_a, p_a, p_b)
        return 0

    lax.fori_loop(0, n_steps // STEP_UNROLL, quad_step, 0)
    add_values(n_steps - 1, p_b)

    def suffix_matrix(n):
        r = lax.broadcasted_iota(jnp.int32, (n, n), 0)
        c = lax.broadcasted_iota(jnp.int32, (n, n), 1)
        return jnp.where(c > r, 1.0, 0.0).astype(BF16)

    def later_sum(x, tri):
        hi = x.astype(BF16)
        lo = (x - hi.astype(F32)).astype(BF16)
        return (jnp.dot(tri, hi, preferred_element_type=F32)
                + jnp.dot(tri, lo, preferred_element_type=F32))

    tri = suffix_matrix(head)
    kh = kh_ref[...]
    vh = vh_ref[...]

    for qi in range(n_blocks):
        z = jnp.dot(kh, qt_ref[qi], preferred_element_type=F32)
        log_beta, log_1m = _log2_sigmoid_pair(z)
        keep = later_sum(log_1m, tri) + rest_ref[qi]
        attn = jnp.exp2(log_beta + keep)
        acc = fin_ref[qi] + _dot_tn(vh, attn.astype(BF16))
        o_ref[pl.ds(head + qi * block, nq), :] = acc.T.astype(o_ref.dtype)

    qt = q_ref[pl.ds(0, head), :].astype(F32)
    qt = jnp.concatenate([qt, jnp.zeros((LANES - head, hw), F32)], axis=0)
    qt = qt.T.astype(BF16)
    z = jnp.dot(kh, qt, preferred_element_type=F32)
    key_h = lax.broadcasted_iota(jnp.int32, (head, LANES), 0)
    qry_h = lax.broadcasted_iota(jnp.int32, (head, LANES), 1)
    z = jnp.where(key_h < qry_h, z, -MASKED)
    log_beta, log_1m = _log2_sigmoid_pair(z)
    attn = jnp.exp2(log_beta + later_sum(log_1m, tri))
    acc = _dot_tn(vh, attn.astype(BF16))
    o_ref[pl.ds(0, head), :] = acc.T[:head].astype(o_ref.dtype)


def sb_attention(q, kv_head, kv_body, *, block=ATTN_BLOCK):
    b, seq_len, d = q.shape
    hw = SB_HEAD_DIM
    n_heads = d // hw
    n_blocks = seq_len // block
    head = seq_len - n_blocks * block
    n_steps = n_blocks * (n_blocks + 1) // 2
    assert n_steps % STEP_UNROLL == 0 and 0 < head <= LANES and head % BF16_SUBLANES == 0
    qi_tab, kj_tab = _causal_schedule(n_blocks, reverse=True)
    smem = pl.BlockSpec(memory_space=pltpu.SMEM)
    body_len = n_blocks * block
    return pl.pallas_call(
        functools.partial(_sb_attn_kernel, seq_len=seq_len, block=block),
        out_shape=jax.ShapeDtypeStruct((b, seq_len, d), BF16),
        grid=(b, n_heads),
        in_specs=[
            smem, smem,
            pl.BlockSpec((None, seq_len, hw), lambda bi, hi: (bi, 0, hi)),
            pl.BlockSpec((None, head, hw), lambda bi, hi: (bi, 0, hi)),
            pl.BlockSpec((None, head, hw), lambda bi, hi: (bi, 0, n_heads + hi)),
            pl.BlockSpec((None, body_len, hw), lambda bi, hi: (bi, 0, hi)),
            pl.BlockSpec((None, body_len, hw), lambda bi, hi: (bi, 0, n_heads + hi)),
        ],
        out_specs=pl.BlockSpec((None, seq_len, hw), lambda bi, hi: (bi, 0, hi)),
        scratch_shapes=[
            pltpu.VMEM((n_blocks, hw, block), BF16),
            pltpu.VMEM((block, block), F32),
            pltpu.VMEM((block, block), F32),
            pltpu.VMEM((block, block), BF16),
            pltpu.VMEM((block, block), BF16),
            pltpu.VMEM((n_blocks + 1, hw, block), F32),
            pltpu.VMEM((n_blocks, 1, block), F32),
            pltpu.VMEM((2, block, block), F32),
        ],
        compiler_params=pltpu.CompilerParams(
            dimension_semantics=("parallel", "parallel"),
            vmem_limit_bytes=VMEM_LIMIT_BYTES),
        name="sb_attention",
    )(qi_tab, kj_tab, q, kv_head, kv_head, kv_body, kv_body)


def kernel(x, meta_tokens, attn_norm_g, mlp_norm_g, da_w_qkv, da_w_o, da_lambda_q1,
           da_lambda_k1, da_lambda_q2, da_lambda_k2, da_subln_g, kv_norm_g, sb_w_k,
           sb_w_v, sb_w_q, sb_w_o, mlp_w_up, mlp_w_down, final_norm_g):
    b, seq, d = x.shape
    n_meta = meta_tokens.shape[0]
    depth = attn_norm_g.shape[0]
    n_a = da_w_qkv.shape[0]
    seq_len = n_meta + seq

    meta = jnp.broadcast_to(meta_tokens.astype(x.dtype)[None], (b, n_meta, d))
    h = jnp.concatenate([meta, x], axis=1).reshape(b * seq_len, d)

    kv = None
    for i in range(depth):
        if i < n_a:
            lambda_init = 0.8 - 0.6 * math.exp(-0.3 * i)
            qkv = norm_matmul(h, attn_norm_g[i], da_w_qkv[i].astype(BF16))
            o = da_attention(qkv.reshape(b, seq_len, 3 * d), da_lambda_q1[i],
                             da_lambda_k1[i], da_lambda_q2[i], da_lambda_k2[i],
                             da_subln_g[i], lambda_init=lambda_init)
            w_o = da_w_o[i]
        else:
            j = i - n_a
            if kv is None:
                w_kv = jnp.concatenate([sb_w_k, sb_w_v], axis=1).astype(BF16)
                kv = norm_matmul(h, kv_norm_g, w_kv).reshape(b, seq_len, 2 * d)
                kv_head, kv_body = interleave_key_rows(kv, seq_len % ATTN_BLOCK,
                                                       ATTN_BLOCK)
            q = norm_matmul(h, attn_norm_g[i], sb_w_q[j].astype(BF16),
                            out_scale=SB_HEAD_DIM ** -0.5 * LOG2_E)
            o = sb_attention(q.reshape(b, seq_len, d), kv_head, kv_body)
            w_o = sb_w_o[j]
        h = attn_out_mlp(h, o.reshape(b * seq_len, d), w_o.astype(BF16), mlp_norm_g[i],
                         mlp_w_up[i].astype(BF16), mlp_w_down[i].astype(BF16))
    return final_norm(h.reshape(b, seq_len, d), final_norm_g, n_meta)
```

```python
import functools
import math

import jax
import jax.numpy as jnp
import numpy as np
from jax import lax
from jax.experimental import pallas as pl
from jax.experimental.pallas import tpu as pltpu

F32 = jnp.float32
BF16 = jnp.bfloat16

RMS_EPS = 1e-6
LOG2_E = math.log2(math.e)
DA_HEAD_DIM = 64
SB_HEAD_DIM = 128
ATTN_BLOCK = 256
STEP_UNROLL = 4
SUBLANES = 8
BF16_SUBLANES = 16
LANES = 128
VMEM_LIMIT_BYTES = 56 * 1024 * 1024


def _pick_tile(n, target, multiple=BF16_SUBLANES):
    best = None
    for t in range(multiple, min(n, target) + 1, multiple):
        if n % t == 0:
            best = t
    if best is None:
        raise ValueError(f"no tile for {n=} {target=}")
    return best


def _rmsnorm_rows(x, g):
    ms = jnp.mean(x * x, axis=-1, keepdims=True)
    return x * lax.rsqrt(ms + RMS_EPS) * g


def _norm_matmul_kernel(h_ref, g_ref, w_ref, o_ref, xn_ref, *, out_scale):
    @pl.when(pl.program_id(1) == 0)
    def _():
        xn_ref[...] = _rmsnorm_rows(h_ref[...], g_ref[...]).astype(BF16)

    acc = jnp.dot(xn_ref[...], w_ref[...], preferred_element_type=F32)
    if out_scale != 1.0:
        acc = acc * out_scale
    o_ref[...] = acc.astype(o_ref.dtype)


def norm_matmul(h, g, w, *, out_scale=1.0, tm_target=768, tn_target=1024):
    m, d = h.shape
    n = w.shape[1]
    tm = _pick_tile(m, tm_target)
    tn = _pick_tile(n, tn_target, LANES)
    return pl.pallas_call(
        functools.partial(_norm_matmul_kernel, out_scale=out_scale),
        out_shape=jax.ShapeDtypeStruct((m, n), BF16),
        grid=(m // tm, n // tn),
        in_specs=[
            pl.BlockSpec((tm, d), lambda i, j: (i, 0)),
            pl.BlockSpec((1, d), lambda i, j: (0, 0)),
            pl.BlockSpec((d, tn), lambda i, j: (0, j)),
        ],
        out_specs=pl.BlockSpec((tm, tn), lambda i, j: (i, j)),
        scratch_shapes=[pltpu.VMEM((tm, d), BF16)],
        compiler_params=pltpu.CompilerParams(
            dimension_semantics=("parallel", "arbitrary"),
            vmem_limit_bytes=VMEM_LIMIT_BYTES),
        name="norm_matmul",
    )(h, g.reshape(1, d), w)


def _attn_out_mlp_kernel(h_ref, o_ref, wo_ref, g_ref, wu_ref, wd_ref, out_ref, xn_ref):
    @pl.when(pl.program_id(1) == 0)
    def _():
        x = h_ref[...] + jnp.dot(o_ref[...], wo_ref[...], preferred_element_type=F32)
        xn_ref[...] = _rmsnorm_rows(x, g_ref[...]).astype(BF16)
        out_ref[...] = x

    a = jnp.dot(xn_ref[...], wu_ref[...], preferred_element_type=F32)
    a = jnp.maximum(a, 0.0)
    a = (a * a).astype(BF16)
    out_ref[...] += jnp.dot(a, wd_ref[...], preferred_element_type=F32)


def attn_out_mlp(h, o, w_o, g, w_up, w_down, *, tm_target=768, tf_target=512):
    m, d = h.shape
    f = w_up.shape[1]
    tm = _pick_tile(m, tm_target)
    tf = _pick_tile(f, tf_target, LANES)
    return pl.pallas_call(
        _attn_out_mlp_kernel,
        out_shape=jax.ShapeDtypeStruct((m, d), F32),
        grid=(m // tm, f // tf),
        in_specs=[
            pl.BlockSpec((tm, d), lambda i, j: (i, 0)),
            pl.BlockSpec((tm, d), lambda i, j: (i, 0)),
            pl.BlockSpec((d, d), lambda i, j: (0, 0), pipeline_mode=pl.Buffered(1)),
            pl.BlockSpec((1, d), lambda i, j: (0, 0)),
            pl.BlockSpec((d, tf), lambda i, j: (0, j)),
            pl.BlockSpec((tf, d), lambda i, j: (j, 0)),
        ],
        out_specs=pl.BlockSpec((tm, d), lambda i, j: (i, 0)),
        scratch_shapes=[pltpu.VMEM((tm, d), BF16)],
        compiler_params=pltpu.CompilerParams(
            dimension_semantics=("parallel", "arbitrary"),
            vmem_limit_bytes=VMEM_LIMIT_BYTES),
        name="attn_out_mlp",
    )(h, o, w_o, g.reshape(1, d), w_up, w_down)


def _final_norm_kernel(h_ref, g_ref, o_ref):
    o_ref[...] = _rmsnorm_rows(h_ref[...], g_ref[...])


def final_norm(h, g, skip, *, tr_target=512):
    b, seq_len, d = h.shape
    rows = seq_len - skip
    assert skip % SUBLANES == 0
    tr = _pick_tile(rows, tr_target, SUBLANES)
    return pl.pallas_call(
        _final_norm_kernel,
        out_shape=jax.ShapeDtypeStruct((b, rows, d), F32),
        grid=(b, rows // tr),
        in_specs=[
            pl.BlockSpec((pl.Element(1), pl.Element(tr), pl.Element(d)),
                         lambda bi, i: (bi, pl.multiple_of(skip + i * tr, SUBLANES), 0)),
            pl.BlockSpec((1, 1, d), lambda bi, i: (0, 0, 0)),
        ],
        out_specs=pl.BlockSpec((1, tr, d), lambda bi, i: (bi, i, 0)),
        compiler_params=pltpu.CompilerParams(
            dimension_semantics=("parallel", "parallel"),
            vmem_limit_bytes=VMEM_LIMIT_BYTES),
        name="final_norm",
    )(h, g.reshape(1, 1, d))


def _dot_tn(a, b):
    return lax.dot_general(a, b, (((0,), (0,)), ((), ())),
                           preferred_element_type=F32)


def _causal_schedule(n_blocks, reverse):
    steps = []
    for qi in range(n_blocks):
        ks = range(qi, -1, -1) if reverse else range(qi + 1)
        steps.extend((qi, kj) for kj in ks)
    steps = [(n_blocks, 0)] * 2 + steps + [steps[-1]]
    qi_tab = np.array([s[0] for s in steps], np.int32)
    kj_tab = np.array([s[1] for s in steps], np.int32)
    return jnp.asarray(qi_tab), jnp.asarray(kj_tab)


N_SLOPE_TERMS = 3
ONES_ROWS = BF16_SUBLANES


def _da_schedule(n_blocks):
    steps = [(qi, kj) for qi in range(n_blocks) for kj in range(qi)]
    steps = [(n_blocks, 0)] + steps + [steps[-1]]
    qi_tab = np.array([s[0] for s in steps], np.int32)
    kj_tab = np.array([s[1] for s in steps], np.int32)
    return jnp.asarray(qi_tab), jnp.asarray(kj_tab)


def _da_attn_kernel(qi_tab, kj_tab, slopes_ref, lq1_ref, lk1_ref, lq2_ref, lk2_ref,
                    q_ref, k_ref, v_ref, g_ref, mask_ref, o_ref,
                    qt_ref, ke1_ref, ke2_ref, vt_ref, s_a, s_b, p_a, p_b, fin_ref, m_ref,
                    *, seq_len, block, lambda_init):
    hd = DA_HEAD_DIM
    hw = 2 * hd
    nq = block
    width = 2 * nq
    n_blocks = seq_len // block
    head = seq_len - n_blocks * block
    n_off = n_blocks * (n_blocks - 1) // 2
    rows_d = block + head
    slope = slopes_ref[pl.program_id(1)] * LOG2_E
    lam = (jnp.exp(jnp.sum(lq1_ref[...] * lk1_ref[...], axis=-1, keepdims=True))
           - jnp.exp(jnp.sum(lq2_ref[...] * lk2_ref[...], axis=-1, keepdims=True))
           + lambda_init)
    scale = hd ** -0.5 * LOG2_E
    out_gain = g_ref[...] * (1.0 - lambda_init)

    def to_f32(i):
        return jnp.asarray(i, jnp.int32).astype(F32)

    sv = jnp.full((1, nq), slope, F32)
    terms = []
    rem = sv
    for _ in range(N_SLOPE_TERMS):
        t_bf = rem.astype(BF16).astype(F32)
        terms.append(t_bf)
        rem = rem - t_bf
    rows = terms + [t * float(block) for t in terms]
    feat_row = lax.broadcasted_iota(jnp.int32, (hw, nq), 0)

    def slope_tile(base):
        tile = jnp.zeros((hw, nq), F32)
        for i, r in enumerate(rows):
            tile = jnp.where(feat_row == base + i, r, tile)
        return tile

    first_map = feat_row < hd
    slopes_1 = slope_tile(hd)
    slopes_2 = slope_tile(0)

    def key_features(n, base, offset, shift):
        lane = lax.broadcasted_iota(jnp.int32, (n, hw), 1) - base
        pos = lax.broadcasted_iota(jnp.int32, (n, hw), 0) + offset
        f = jnp.where((lane >= 0) & (lane < N_SLOPE_TERMS), pos, 0)
        f = jnp.where((lane >= N_SLOPE_TERMS) & (lane < 2 * N_SLOPE_TERMS), shift, f)
        return f.astype(F32).astype(BF16)

    key_lane = lax.broadcasted_iota(jnp.int32, (block, hw), 1)
    feat_1 = key_features(block, hd, 0, 0)
    feat_2 = key_features(block, 0, 0, 0)
    ones_rows = jnp.ones((ONES_ROWS, block), BF16)
    for kb in range(n_blocks):
        r0 = head + kb * block
        k = k_ref[pl.ds(r0, block), :]
        ke1_ref[kb] = jnp.where(key_lane < hd, k, feat_1)
        ke2_ref[kb] = jnp.where(key_lane < hd, feat_2, k)
        vt_ref[kb, pl.ds(0, hw), :] = v_ref[pl.ds(r0, block), :].astype(F32).T.astype(BF16)
        vt_ref[kb, pl.ds(hw, ONES_ROWS), :] = ones_rows

    for qi in range(n_blocks):
        q0 = head + qi * block
        qt = (q_ref[pl.ds(q0, nq), :].astype(F32) * scale).T
        qt_ref[qi] = jnp.concatenate([jnp.where(first_map, qt, slopes_1),
                                      jnp.where(first_map, slopes_2, qt)],
                                     axis=1).astype(BF16)
        m_ref[qi] = jnp.full((1, width), -jnp.inf, F32)
        fin_ref[qi] = jnp.zeros((hw + ONES_ROWS, width), F32)
    fin_ref[n_blocks] = jnp.zeros((hw + ONES_ROWS, width), F32)

    def two_map_scores(k1, k2, qt):
        return jnp.concatenate(
            [jnp.dot(k1, qt[:, :nq], preferred_element_type=F32),
             jnp.dot(k2, qt[:, nq:], preferred_element_type=F32)], axis=1)

    def softmax_update(qi, s, c, p_out):
        m_old = m_ref[qi]
        m_new = jnp.maximum(m_old, jnp.max(s, axis=0, keepdims=True) + c)
        alpha = jnp.exp2(m_old - m_new)
        p_out[...] = jnp.exp2(s - (m_new - c)).astype(BF16)
        m_ref[qi] = m_new
        fin_ref[qi] = alpha * fin_ref[qi]

    top = pl.ds(0, block)

    def scores(t, s_out):
        kj = kj_tab[t + 1]
        s_out[top, :] = two_map_scores(ke1_ref[kj], ke2_ref[kj], qt_ref[qi_tab[t + 1]])

    def add_values(t, p_in):
        qi = qi_tab[t + 1]
        fin_ref[qi] = fin_ref[qi] + jnp.dot(vt_ref[kj_tab[t + 1]], p_in[top, :],
                                            preferred_element_type=F32)

    def step(t, s_cur, s_next, p_prev, p_cur):
        scores(t + 1, s_next)
        add_values(t - 1, p_prev)
        qi = qi_tab[t + 1]
        kj = kj_tab[t + 1]
        c = slope * to_f32((kj - qi) * block)
        softmax_update(qi, s_cur[top, :], c, p_cur.at[top])

    scores(0, s_a)
    p_b[...] = jnp.zeros((rows_d, width), BF16)

    def quad_step(u, _):
        step(4 * u, s_a, s_b, p_b, p_a)
        step(4 * u + 1, s_b, s_a, p_a, p_b)
        step(4 * u + 2, s_a, s_b, p_b, p_a)
        step(4 * u + 3, s_b, s_a, p_a, p_b)
        return 0

    lax.fori_loop(0, n_off // STEP_UNROLL, quad_step, 0)

    if head:
        kh = k_ref[pl.ds(0, head), :]
        head_lane = lax.broadcasted_iota(jnp.int32, (head, hw), 1)
        vh = jnp.concatenate([v_ref[pl.ds(0, head), :].astype(F32),
                              jnp.zeros((LANES - head, hw), F32)], axis=0)
        vt_head = jnp.concatenate([vh.T[:, :head].astype(BF16),
                                   jnp.ones((ONES_ROWS, head), BF16)], axis=0)

    def diag_scores(qi, s_out):
        k1, k2 = ke1_ref[qi], ke2_ref[qi]
        if head:
            f1 = key_features(head, hd, block - head, -(qi + 1))
            f2 = key_features(head, 0, block - head, -(qi + 1))
            k1 = jnp.concatenate([k1, jnp.where(head_lane < hd, kh, f1)], axis=0)
            k2 = jnp.concatenate([k2, jnp.where(head_lane < hd, f2, kh)], axis=0)
        s_out[...] = two_map_scores(k1, k2, qt_ref[qi])

    def diag_add_values(qi, p_in):
        vt = vt_ref[qi]
        if head:
            vt = jnp.concatenate([vt, vt_head], axis=1)
        fin_ref[qi] = fin_ref[qi] + jnp.dot(vt, p_in[...], preferred_element_type=F32)

    bufs = ((s_a, p_a), (s_b, p_b))
    diag_scores(0, s_a)
    for qi in range(n_blocks):
        (s_cur, p_cur), (s_next, p_prev) = bufs[qi % 2], bufs[(qi + 1) % 2]
        if qi + 1 < n_blocks:
            diag_scores(qi + 1, s_next)
        if qi == 0:
            add_values(n_off - 1, p_prev)
        else:
            diag_add_values(qi - 1, p_prev)
        softmax_update(qi, s_cur[...] + mask_ref[...], 0.0, p_cur)
    diag_add_values(n_blocks - 1, bufs[(n_blocks - 1) % 2][1])

    def finish_rows(acc, l, nrows):
        inv = 1.0 / l
        o = acc[:, :nrows] * inv[:, :nrows] - acc[:, nrows:] * (lam * inv[:, nrows:])
        ms = jnp.mean(o * o, axis=0, keepdims=True)
        return (o * lax.rsqrt(ms + RMS_EPS)).T * out_gain

    for qi in range(n_blocks):
        acc = fin_ref[qi]
        o = finish_rows(acc[:hw], acc[hw:hw + 1], nq)
        o_ref[pl.ds(head + qi * block, nq), :] = o.astype(o_ref.dtype)

    if head:
        n_pad = LANES // 2
        q = q_ref[pl.ds(0, head), :].astype(F32) * scale
        q = jnp.concatenate([q, jnp.zeros((n_pad - head, hw), F32)], axis=0)
        first = lax.broadcasted_iota(jnp.int32, (hw, n_pad), 0) < hd
        qt = q.T
        qt = jnp.concatenate([jnp.where(first, qt, 0.0), jnp.where(first, 0.0, qt)],
                             axis=1).astype(BF16)
        s = jnp.dot(kh, qt, preferred_element_type=F32)
        key_h = lax.broadcasted_iota(jnp.int32, (head, 2 * n_pad), 0)
        qry_h = lax.broadcasted_iota(jnp.int32, (head, 2 * n_pad), 1)
        qry_h = jnp.where(qry_h >= n_pad, qry_h - n_pad, qry_h)
        s = jnp.where(key_h <= qry_h, s + slope * key_h.astype(F32), -jnp.inf)
        p = jnp.exp2(s - jnp.max(s, axis=0, keepdims=True))
        l = jnp.sum(p, axis=0, keepdims=True)
        acc = _dot_tn(v_ref[pl.ds(0, head), :], p.astype(BF16))
        o = finish_rows(acc, l, n_pad)
        o_ref[pl.ds(0, head), :] = o[:head].astype(o_ref.dtype)


def _da_diag_mask(block, head):
    key = np.arange(block + head)[:, None]
    qry = np.arange(2 * block)[None, :] % block
    visible = (key <= qry) | (key >= block)
    return jnp.asarray(np.where(visible, 0.0, -np.inf).astype(np.float32))


def da_attention(qkv, lam_q1, lam_k1, lam_q2, lam_k2, subln_g, *, lambda_init,
                 block=ATTN_BLOCK):
    b, seq_len, three_d = qkv.shape
    d = three_d // 3
    hw = 2 * DA_HEAD_DIM
    n_heads = d // hw
    n_blocks = seq_len // block
    n_off = n_blocks * (n_blocks - 1) // 2
    head = seq_len - n_blocks * block
    assert n_off % STEP_UNROLL == 0 and head <= LANES // 2 and head % BF16_SUBLANES == 0
    assert block <= 256
    qi_tab, kj_tab = _da_schedule(n_blocks)
    slopes = jnp.asarray(
        np.array([2.0 ** (-8.0 * (i + 1) / n_heads) for i in range(n_heads)],
                 dtype=np.float32))
    vec = lambda a: a.reshape(1, -1).astype(F32)
    small = lambda n: pl.BlockSpec((1, n), lambda bi, hi: (0, 0))
    smem = pl.BlockSpec(memory_space=pltpu.SMEM)
    width = 2 * block
    rows_d = block + head
    return pl.pallas_call(
        functools.partial(_da_attn_kernel, seq_len=seq_len, block=block,
                          lambda_init=lambda_init),
        out_shape=jax.ShapeDtypeStruct((b, seq_len, d), BF16),
        grid=(b, n_heads),
        in_specs=[
            smem, smem, smem,
            small(DA_HEAD_DIM), small(DA_HEAD_DIM), small(DA_HEAD_DIM),
            small(DA_HEAD_DIM),
            pl.BlockSpec((None, seq_len, hw), lambda bi, hi: (bi, 0, hi)),
            pl.BlockSpec((None, seq_len, hw), lambda bi, hi: (bi, 0, n_heads + hi)),
            pl.BlockSpec((None, seq_len, hw),
                         lambda bi, hi: (bi, 0, 2 * n_heads + hi)),
            small(hw),
            pl.BlockSpec((rows_d, width), lambda bi, hi: (0, 0)),
        ],
        out_specs=pl.BlockSpec((None, seq_len, hw), lambda bi, hi: (bi, 0, hi)),
        scratch_shapes=[
            pltpu.VMEM((n_blocks, hw, width), BF16),
            pltpu.VMEM((n_blocks, block, hw), BF16),
            pltpu.VMEM((n_blocks, block, hw), BF16),
            pltpu.VMEM((n_blocks, hw + ONES_ROWS, block), BF16),
            pltpu.VMEM((rows_d, width), F32),
            pltpu.VMEM((rows_d, width), F32),
            pltpu.VMEM((rows_d, width), BF16),
            pltpu.VMEM((rows_d, width), BF16),
            pltpu.VMEM((n_blocks + 1, hw + ONES_ROWS, width), F32),
            pltpu.VMEM((n_blocks, 1, width), F32),
        ],
        compiler_params=pltpu.CompilerParams(
            dimension_semantics=("parallel", "parallel"),
            vmem_limit_bytes=VMEM_LIMIT_BYTES),
        name="da_attention",
    )(qi_tab, kj_tab, slopes, vec(lam_q1), vec(lam_k1), vec(lam_q2), vec(lam_k2),
      qkv, qkv, qkv, vec(subln_g), _da_diag_mask(block, head))


MASKED = 1e30


def interleave_key_rows(x, head, block):
    b, seq_len, c = x.shape
    n_blocks = (seq_len - head) // block
    body = x[:, head:].reshape(b, n_blocks, SUBLANES, block // SUBLANES, c)
    body = jnp.swapaxes(body, 2, 3).reshape(b, n_blocks * block, c)
    return x[:, :head], body


def _log2_sigmoid_pair(z2):
    sign = jnp.int32(-2 ** 31)
    neg_abs = lax.bitcast_convert_type(lax.bitcast_convert_type(z2, jnp.int32) | sign, F32)
    sp = jnp.log2(1.0 + jnp.exp2(neg_abs))
    log_beta = jnp.minimum(z2, 0.0) - sp
    return log_beta, log_beta - z2


def _sb_attn_kernel(qi_tab, kj_tab, q_ref, kh_ref, vh_ref, k_ref, v_ref, o_ref,
                    qt_ref, s_a, s_b, p_a, p_b, fin_ref, rest_ref, mask_ref,
                    *, seq_len, block):
    hw = SB_HEAD_DIM
    nq = block
    n_blocks = seq_len // block
    head = seq_len - n_blocks * block
    n_steps = n_blocks * (n_blocks + 1) // 2
    groups = block // SUBLANES

    row = lax.broadcasted_iota(jnp.int32, (block, nq), 0)
    qry = lax.broadcasted_iota(jnp.int32, (block, nq), 1)
    key = (row >> int(math.log2(SUBLANES))) + groups * (row & (SUBLANES - 1))
    mask_ref[0] = jnp.zeros((block, nq), F32)
    mask_ref[1] = jnp.where(key < qry, 0.0, -MASKED)

    for qi in range(n_blocks):
        q0 = head + qi * block
        qt_ref[qi] = q_ref[pl.ds(q0, nq), :].astype(F32).T.astype(BF16)
        rest_ref[qi] = jnp.zeros((1, nq), F32)
        fin_ref[qi] = jnp.zeros((hw, nq), F32)
    fin_ref[n_blocks] = jnp.zeros((hw, nq), F32)

    def key_rows(kj):
        return pl.ds(pl.multiple_of(kj * block, block), block)

    def scores(t, s_out):
        s_out[...] = jnp.dot(k_ref[key_rows(kj_tab[t + 2]), :], qt_ref[qi_tab[t + 2]],
                             preferred_element_type=F32)

    def add_values(t, p_in):
        qi = qi_tab[t + 2]
        fin_ref[qi] = fin_ref[qi] + _dot_tn(v_ref[key_rows(kj_tab[t + 2]), :], p_in[...])

    sub = lax.broadcasted_iota(jnp.int32, (SUBLANES, nq), 0)

    def step(t, s_cur, s_next, p_buf):
        scores(t + 1, s_next)
        add_values(t - 2, p_buf)
        qi = qi_tab[t + 2]
        kj = kj_tab[t + 2]
        z = s_cur[...] + mask_ref[(qi == kj).astype(jnp.int32)]
        log_beta, log_1m = _log2_sigmoid_pair(z)
        later = [None] * groups
        run = jnp.zeros((SUBLANES, nq), F32)
        for r in reversed(range(groups)):
            later[r] = run
            run = run + log_1m[r * SUBLANES:(r + 1) * SUBLANES, :]
        above = jnp.zeros((SUBLANES, nq), F32)
        for d in range(1, SUBLANES):
            shifted = pltpu.roll(run, SUBLANES - d, 0)
            above = above + jnp.where(sub < SUBLANES - d, shifted, 0.0)
        rest = rest_ref[qi]
        base = above + rest
        keep = jnp.concatenate([g + base for g in later], axis=0)
        p_buf[...] = jnp.exp2(log_beta + keep).astype(BF16)
        rest_ref[qi] = rest + (above + run)[0:1, :]

    scores(0, s_a)
    p_a[...] = jnp.zeros((block, nq), BF16)
    p_b[...] = jnp.zeros((block, nq), BF16)

    def quad_step(u, _):
        step(4 * u, s_a, s_b, p_a)
        step(4 * u + 1, s_b, s_a, p_b)
        step(4 * u + 2, s_a, s_b, p_a)
        step(4 * u + 3, s_b, s_a, p_b)
        return 0

    lax.fori_loop(0, n_steps // STEP_UNROLL, quad_step, 0)
    add_values(n_steps - 2, p_a)
    add_values(n_steps - 1, p_b)

    def suffix_matrix(n):
        r = lax.broadcasted_iota(jnp.int32, (n, n), 0)
        c = lax.broadcasted_iota(jnp.int32, (n, n), 1)
        return jnp.where(c > r, 1.0, 0.0).astype(BF16)

    def later_sum(x, tri):
        hi = x.astype(BF16)
        lo = (x - hi.astype(F32)).astype(BF16)
        return (jnp.dot(tri, hi, preferred_element_type=F32)
                + jnp.dot(tri, lo, preferred_element_type=F32))

    tri = suffix_matrix(head)
    kh = kh_ref[...]
    vh = vh_ref[...]

    for qi in range(n_blocks):
        z = jnp.dot(kh, qt_ref[qi], preferred_element_type=F32)
        log_beta, log_1m = _log2_sigmoid_pair(z)
        keep = later_sum(log_1m, tri) + rest_ref[qi]
        attn = jnp.exp2(log_beta + keep)
        acc = fin_ref[qi] + _dot_tn(vh, attn.astype(BF16))
        o_ref[pl.ds(head + qi * block, nq), :] = acc.T.astype(o_ref.dtype)

    qt = q_ref[pl.ds(0, head), :].astype(F32)
    qt = jnp.concatenate([qt, jnp.zeros((LANES - head, hw), F32)], axis=0)
    qt = qt.T.astype(BF16)
    z = jnp.dot(kh, qt, preferred_element_type=F32)
    key_h = lax.broadcasted_iota(jnp.int32, (head, LANES), 0)
    qry_h = lax.broadcasted_iota(jnp.int32, (head, LANES), 1)
    z = jnp.where(key_h < qry_h, z, -MASKED)
    log_beta, log_1m = _log2_sigmoid_pair(z)
    attn = jnp.exp2(log_beta + later_sum(log_1m, tri))
    acc = _dot_tn(vh, attn.astype(BF16))
    o_ref[pl.ds(0, head), :] = acc.T[:head].astype(o_ref.dtype)


def sb_attention(q, kv_head, kv_body, *, block=ATTN_BLOCK):
    b, seq_len, d = q.shape
    hw = SB_HEAD_DIM
    n_heads = d // hw
    n_blocks = seq_len // block
    head = seq_len - n_blocks * block
    n_steps = n_blocks * (n_blocks + 1) // 2
    assert n_steps % STEP_UNROLL == 0 and 0 < head <= LANES and head % BF16_SUBLANES == 0
    qi_tab, kj_tab = _causal_schedule(n_blocks, reverse=True)
    smem = pl.BlockSpec(memory_space=pltpu.SMEM)
    body_len = n_blocks * block
    return pl.pallas_call(
        functools.partial(_sb_attn_kernel, seq_len=seq_len, block=block),
        out_shape=jax.ShapeDtypeStruct((b, seq_len, d), BF16),
        grid=(b, n_heads),
        in_specs=[
            smem, smem,
            pl.BlockSpec((None, seq_len, hw), lambda bi, hi: (bi, 0, hi)),
            pl.BlockSpec((None, head, hw), lambda bi, hi: (bi, 0, hi)),
            pl.BlockSpec((None, head, hw), lambda bi, hi: (bi, 0, n_heads + hi)),
            pl.BlockSpec((None, body_len, hw), lambda bi, hi: (bi, 0, hi)),
            pl.BlockSpec((None, body_len, hw), lambda bi, hi: (bi, 0, n_heads + hi)),
        ],
        out_specs=pl.BlockSpec((None, seq_len, hw), lambda bi, hi: (bi, 0, hi)),
        scratch_shapes=[
            pltpu.VMEM((n_blocks, hw, block), BF16),
            pltpu.VMEM((block, block), F32),
            pltpu.VMEM((block, block), F32),
            pltpu.VMEM((block, block), BF16),
            pltpu.VMEM((block, block), BF16),
            pltpu.VMEM((n_blocks + 1, hw, block), F32),
            pltpu.VMEM((n_blocks, 1, block), F32),
            pltpu.VMEM((2, block, block), F32),
        ],
        compiler_params=pltpu.CompilerParams(
            dimension_semantics=("parallel", "parallel"),
            vmem_limit_bytes=VMEM_LIMIT_BYTES),
        name="sb_attention",
    )(qi_tab, kj_tab, q, kv_head, kv_head, kv_body, kv_body)


def kernel(x, meta_tokens, attn_norm_g, mlp_norm_g, da_w_qkv, da_w_o, da_lambda_q1,
           da_lambda_k1, da_lambda_q2, da_lambda_k2, da_subln_g, kv_norm_g, sb_w_k,
           sb_w_v, sb_w_q, sb_w_o, mlp_w_up, mlp_w_down, final_norm_g):
    b, seq, d = x.shape
    n_meta = meta_tokens.shape[0]
    depth = attn_norm_g.shape[0]
    n_a = da_w_qkv.shape[0]
    seq_len = n_meta + seq

    meta = jnp.broadcast_to(meta_tokens.astype(x.dtype)[None], (b, n_meta, d))
    h = jnp.concatenate([meta, x], axis=1).reshape(b * seq_len, d)

    kv = None
    for i in range(depth):
        if i < n_a:
            lambda_init = 0.8 - 0.6 * math.exp(-0.3 * i)
            qkv = norm_matmul(h, attn_norm_g[i], da_w_qkv[i].astype(BF16))
            o = da_attention(qkv.reshape(b, seq_len, 3 * d), da_lambda_q1[i],
                             da_lambda_k1[i], da_lambda_q2[i], da_lambda_k2[i],
                             da_subln_g[i], lambda_init=lambda_init)
            w_o = da_w_o[i]
        else:
            j = i - n_a
            if kv is None:
                w_kv = jnp.concatenate([sb_w_k, sb_w_v], axis=1).astype(BF16)
                kv = norm_matmul(h, kv_norm_g, w_kv).reshape(b, seq_len, 2 * d)
                kv_head, kv_body = interleave_key_rows(kv, seq_len % ATTN_BLOCK,
                                                       ATTN_BLOCK)
            q = norm_matmul(h, attn_norm_g[i], sb_w_q[j].astype(BF16),
                            out_scale=SB_HEAD_DIM ** -0.5 * LOG2_E)
            o = sb_attention(q.reshape(b, seq_len, d), kv_head, kv_body)
            w_o = sb_w_o[j]
        h = attn_out_mlp(h, o.reshape(b * seq_len, d), w_o.astype(BF16), mlp_norm_g[i],
                         mlp_w_up[i].astype(BF16), mlp_w_down[i].astype(BF16))
    return final_norm(h.reshape(b, seq_len, d), final_norm_g, n_meta)
```

```python
import functools
import math

import jax
import jax.numpy as jnp
import numpy as np
from jax import lax
from jax.experimental import pallas as pl
from jax.experimental.pallas import tpu as pltpu

F32 = jnp.float32
BF16 = jnp.bfloat16

RMS_EPS = 1e-6
LOG2_E = math.log2(math.e)
DA_HEAD_DIM = 64
SB_HEAD_DIM = 128
ATTN_BLOCK = 256
STEP_UNROLL = 4
SUBLANES = 8
BF16_SUBLANES = 16
LANES = 128
VMEM_LIMIT_BYTES = 56 * 1024 * 1024


def _pick_tile(n, target, multiple=BF16_SUBLANES):
    best = None
    for t in range(multiple, min(n, target) + 1, multiple):
        if n % t == 0:
            best = t
    if best is None:
        raise ValueError(f"no tile for {n=} {target=}")
    return best


def _rmsnorm_rows(x, g):
    ms = jnp.mean(x * x, axis=-1, keepdims=True)
    return x * lax.rsqrt(ms + RMS_EPS) * g


def _norm_matmul_kernel(h_ref, g_ref, w_ref, o_ref, xn_ref, *, out_scale):
    @pl.when(pl.program_id(1) == 0)
    def _():
        xn_ref[...] = _rmsnorm_rows(h_ref[...], g_ref[...]).astype(BF16)

    acc = jnp.dot(xn_ref[...], w_ref[...], preferred_element_type=F32)
    if out_scale != 1.0:
        acc = acc * out_scale
    o_ref[...] = acc.astype(o_ref.dtype)


def norm_matmul(h, g, w, *, out_scale=1.0, tm_target=768, tn_target=2048):
    m, d = h.shape
    n = w.shape[1]
    tm = _pick_tile(m, tm_target)
    tn = _pick_tile(n, tn_target, LANES)
    return pl.pallas_call(
        functools.partial(_norm_matmul_kernel, out_scale=out_scale),
        out_shape=jax.ShapeDtypeStruct((m, n), BF16),
        grid=(m // tm, n // tn),
        in_specs=[
            pl.BlockSpec((tm, d), lambda i, j: (i, 0)),
            pl.BlockSpec((1, d), lambda i, j: (0, 0)),
            pl.BlockSpec((d, tn), lambda i, j: (0, j)),
        ],
        out_specs=pl.BlockSpec((tm, tn), lambda i, j: (i, j)),
        scratch_shapes=[pltpu.VMEM((tm, d), BF16)],
        compiler_params=pltpu.CompilerParams(
            dimension_semantics=("parallel", "arbitrary"),
            vmem_limit_bytes=VMEM_LIMIT_BYTES),
        name="norm_matmul",
    )(h, g.reshape(1, d), w)


def _attn_out_mlp_kernel(h_ref, o_ref, wo_ref, g_ref, wu_ref, wd_ref, out_ref, xn_ref):
    @pl.when(pl.program_id(1) == 0)
    def _():
        x = h_ref[...] + jnp.dot(o_ref[...], wo_ref[...], preferred_element_type=F32)
        xn_ref[...] = _rmsnorm_rows(x, g_ref[...]).astype(BF16)
        out_ref[...] = x

    a = jnp.dot(xn_ref[...], wu_ref[...], preferred_element_type=F32)
    a = jnp.maximum(a, 0.0)
    a = (a * a).astype(BF16)
    out_ref[...] += jnp.dot(a, wd_ref[...], preferred_element_type=F32)


def attn_out_mlp(h, o, w_o, g, w_up, w_down, *, tm_target=768, tf_target=512):
    m, d = h.shape
    f = w_up.shape[1]
    tm = _pick_tile(m, tm_target)
    tf = _pick_tile(f, tf_target, LANES)
    return pl.pallas_call(
        _attn_out_mlp_kernel,
        out_shape=jax.ShapeDtypeStruct((m, d), F32),
        grid=(m // tm, f // tf),
        in_specs=[
            pl.BlockSpec((tm, d), lambda i, j: (i, 0)),
            pl.BlockSpec((tm, d), lambda i, j: (i, 0)),
            pl.BlockSpec((d, d), lambda i, j: (0, 0), pipeline_mode=pl.Buffered(1)),
            pl.BlockSpec((1, d), lambda i, j: (0, 0)),
            pl.BlockSpec((d, tf), lambda i, j: (0, j)),
            pl.BlockSpec((tf, d), lambda i, j: (j, 0)),
        ],
        out_specs=pl.BlockSpec((tm, d), lambda i, j: (i, 0)),
        scratch_shapes=[pltpu.VMEM((tm, d), BF16)],
        compiler_params=pltpu.CompilerParams(
            dimension_semantics=("parallel", "arbitrary"),
            vmem_limit_bytes=VMEM_LIMIT_BYTES),
        name="attn_out_mlp",
    )(h, o, w_o, g.reshape(1, d), w_up, w_down)


def _final_norm_kernel(h_ref, g_ref, o_ref):
    o_ref[...] = _rmsnorm_rows(h_ref[...], g_ref[...])


def final_norm(h, g, skip, *, tr_target=512):
    b, seq_len, d = h.shape
    rows = seq_len - skip
    assert skip % SUBLANES == 0
    tr = _pick_tile(rows, tr_target, SUBLANES)
    return pl.pallas_call(
        _final_norm_kernel,
        out_shape=jax.ShapeDtypeStruct((b, rows, d), F32),
        grid=(b, rows // tr),
        in_specs=[
            pl.BlockSpec((pl.Element(1), pl.Element(tr), pl.Element(d)),
                         lambda bi, i: (bi, pl.multiple_of(skip + i * tr, SUBLANES), 0)),
            pl.BlockSpec((1, 1, d), lambda bi, i: (0, 0, 0)),
        ],
        out_specs=pl.BlockSpec((1, tr, d), lambda bi, i: (bi, i, 0)),
        compiler_params=pltpu.CompilerParams(
            dimension_semantics=("parallel", "parallel"),
            vmem_limit_bytes=VMEM_LIMIT_BYTES),
        name="final_norm",
    )(h, g.reshape(1, 1, d))


def _dot_tn(a, b):
    return lax.dot_general(a, b, (((0,), (0,)), ((), ())),
                           preferred_element_type=F32)


def _sb_schedule(n_blocks):
    steps = [(qi, qi) for qi in range(n_blocks)]
    steps += [(qi, kj) for qi in range(n_blocks) for kj in range(qi - 1, -1, -1)]
    steps = [(n_blocks, 0)] * 2 + steps + [steps[-1]]
    qi_tab = np.array([s[0] for s in steps], np.int32)
    kj_tab = np.array([s[1] for s in steps], np.int32)
    return jnp.asarray(qi_tab), jnp.asarray(kj_tab)


N_SLOPE_TERMS = 3
ONES_ROWS = BF16_SUBLANES


def _da_schedule(n_blocks):
    steps = [(qi, kj) for qi in range(n_blocks) for kj in range(qi)]
    steps = [(n_blocks, 0)] + steps + [steps[-1]]
    qi_tab = np.array([s[0] for s in steps], np.int32)
    kj_tab = np.array([s[1] for s in steps], np.int32)
    return jnp.asarray(qi_tab), jnp.asarray(kj_tab)


def _da_attn_kernel(qi_tab, kj_tab, slopes_ref, lq1_ref, lk1_ref, lq2_ref, lk2_ref,
                    q_ref, k_ref, v_ref, g_ref, mask_ref, o_ref,
                    qt_ref, ke1_ref, ke2_ref, vt_ref, s_a, s_b, p_a, p_b, fin_ref, m_ref,
                    *, seq_len, block, lambda_init):
    hd = DA_HEAD_DIM
    hw = 2 * hd
    nq = block
    width = 2 * nq
    n_blocks = seq_len // block
    head = seq_len - n_blocks * block
    n_off = n_blocks * (n_blocks - 1) // 2
    rows_d = block + head
    slope = slopes_ref[pl.program_id(1)] * LOG2_E
    lam = (jnp.exp(jnp.sum(lq1_ref[...] * lk1_ref[...], axis=-1, keepdims=True))
           - jnp.exp(jnp.sum(lq2_ref[...] * lk2_ref[...], axis=-1, keepdims=True))
           + lambda_init)
    scale = hd ** -0.5 * LOG2_E
    out_gain = g_ref[...] * (1.0 - lambda_init)

    def to_f32(i):
        return jnp.asarray(i, jnp.int32).astype(F32)

    sv = jnp.full((1, nq), slope, F32)
    terms = []
    rem = sv
    for _ in range(N_SLOPE_TERMS):
        t_bf = rem.astype(BF16).astype(F32)
        terms.append(t_bf)
        rem = rem - t_bf
    rows = terms + [t * float(block) for t in terms]
    feat_row = lax.broadcasted_iota(jnp.int32, (hw, nq), 0)

    def slope_tile(base):
        tile = jnp.zeros((hw, nq), F32)
        for i, r in enumerate(rows):
            tile = jnp.where(feat_row == base + i, r, tile)
        return tile

    first_map = feat_row < hd
    slopes_1 = slope_tile(hd)
    slopes_2 = slope_tile(0)

    def key_features(n, base, offset, shift):
        lane = lax.broadcasted_iota(jnp.int32, (n, hw), 1) - base
        pos = lax.broadcasted_iota(jnp.int32, (n, hw), 0) + offset
        f = jnp.where((lane >= 0) & (lane < N_SLOPE_TERMS), pos, 0)
        f = jnp.where((lane >= N_SLOPE_TERMS) & (lane < 2 * N_SLOPE_TERMS), shift, f)
        return f.astype(F32).astype(BF16)

    key_lane = lax.broadcasted_iota(jnp.int32, (block, hw), 1)
    feat_1 = key_features(block, hd, 0, 0)
    feat_2 = key_features(block, 0, 0, 0)
    ones_rows = jnp.ones((ONES_ROWS, block), BF16)
    for kb in range(n_blocks):
        r0 = head + kb * block
        k = k_ref[pl.ds(r0, block), :]
        ke1_ref[kb] = jnp.where(key_lane < hd, k, feat_1)
        ke2_ref[kb] = jnp.where(key_lane < hd, feat_2, k)
        vt_ref[kb, pl.ds(0, hw), :] = v_ref[pl.ds(r0, block), :].astype(F32).T.astype(BF16)
        vt_ref[kb, pl.ds(hw, ONES_ROWS), :] = ones_rows

    for qi in range(n_blocks):
        q0 = head + qi * block
        qt = (q_ref[pl.ds(q0, nq), :].astype(F32) * scale).T
        qt_ref[qi] = jnp.concatenate([jnp.where(first_map, qt, slopes_1),
                                      jnp.where(first_map, slopes_2, qt)],
                                     axis=1).astype(BF16)
        m_ref[qi] = jnp.full((1, width), -jnp.inf, F32)
        fin_ref[qi] = jnp.zeros((hw + ONES_ROWS, width), F32)
    fin_ref[n_blocks] = jnp.zeros((hw + ONES_ROWS, width), F32)

    def two_map_scores(k1, k2, qt):
        return jnp.concatenate(
            [jnp.dot(k1, qt[:, :nq], preferred_element_type=F32),
             jnp.dot(k2, qt[:, nq:], preferred_element_type=F32)], axis=1)

    def softmax_update(qi, s, c, p_out):
        m_old = m_ref[qi]
        m_new = jnp.maximum(m_old, jnp.max(s, axis=0, keepdims=True) + c)
        alpha = jnp.exp2(m_old - m_new)
        p_out[...] = jnp.exp2(s - (m_new - c)).astype(BF16)
        m_ref[qi] = m_new
        fin_ref[qi] = alpha * fin_ref[qi]

    top = pl.ds(0, block)

    def scores(t, s_out):
        kj = kj_tab[t + 1]
        s_out[top, :] = two_map_scores(ke1_ref[kj], ke2_ref[kj], qt_ref[qi_tab[t + 1]])

    def add_values(t, p_in):
        qi = qi_tab[t + 1]
        fin_ref[qi] = fin_ref[qi] + jnp.dot(vt_ref[kj_tab[t + 1]], p_in[top, :],
                                            preferred_element_type=F32)

    def step(t, s_cur, s_next, p_prev, p_cur):
        scores(t + 1, s_next)
        add_values(t - 1, p_prev)
        qi = qi_tab[t + 1]
        kj = kj_tab[t + 1]
        c = slope * to_f32((kj - qi) * block)
        softmax_update(qi, s_cur[top, :], c, p_cur.at[top])

    scores(0, s_a)
    p_b[...] = jnp.zeros((rows_d, width), BF16)

    def quad_step(u, _):
        step(4 * u, s_a, s_b, p_b, p_a)
        step(4 * u + 1, s_b, s_a, p_a, p_b)
        step(4 * u + 2, s_a, s_b, p_b, p_a)
        step(4 * u + 3, s_b, s_a, p_a, p_b)
        return 0

    lax.fori_loop(0, n_off // STEP_UNROLL, quad_step, 0)

    if head:
        kh = k_ref[pl.ds(0, head), :]
        head_lane = lax.broadcasted_iota(jnp.int32, (head, hw), 1)
        vh = jnp.concatenate([v_ref[pl.ds(0, head), :].astype(F32),
                              jnp.zeros((LANES - head, hw), F32)], axis=0)
        vt_head = jnp.concatenate([vh.T[:, :head].astype(BF16),
                                   jnp.ones((ONES_ROWS, head), BF16)], axis=0)

    def diag_scores(qi, s_out):
        k1, k2 = ke1_ref[qi], ke2_ref[qi]
        if head:
            f1 = key_features(head, hd, block - head, -(qi + 1))
            f2 = key_features(head, 0, block - head, -(qi + 1))
            k1 = jnp.concatenate([k1, jnp.where(head_lane < hd, kh, f1)], axis=0)
            k2 = jnp.concatenate([k2, jnp.where(head_lane < hd, f2, kh)], axis=0)
        s_out[...] = two_map_scores(k1, k2, qt_ref[qi])

    def diag_add_values(qi, p_in):
        vt = vt_ref[qi]
        if head:
            vt = jnp.concatenate([vt, vt_head], axis=1)
        fin_ref[qi] = fin_ref[qi] + jnp.dot(vt, p_in[...], preferred_element_type=F32)

    bufs = ((s_a, p_a), (s_b, p_b))
    diag_scores(0, s_a)
    for qi in range(n_blocks):
        (s_cur, p_cur), (s_next, p_prev) = bufs[qi % 2], bufs[(qi + 1) % 2]
        if qi + 1 < n_blocks:
            diag_scores(qi + 1, s_next)
        if qi == 0:
            add_values(n_off - 1, p_prev)
        else:
            diag_add_values(qi - 1, p_prev)
        softmax_update(qi, s_cur[...] + mask_ref[...], 0.0, p_cur)
    diag_add_values(n_blocks - 1, bufs[(n_blocks - 1) % 2][1])

    def finish_rows(acc, l, nrows):
        inv = 1.0 / l
        o = acc[:, :nrows] * inv[:, :nrows] - acc[:, nrows:] * (lam * inv[:, nrows:])
        ms = jnp.mean(o * o, axis=0, keepdims=True)
        return (o * lax.rsqrt(ms + RMS_EPS)).T * out_gain

    for qi in range(n_blocks):
        acc = fin_ref[qi]
        o = finish_rows(acc[:hw], acc[hw:hw + 1], nq)
        o_ref[pl.ds(head + qi * block, nq), :] = o.astype(o_ref.dtype)

    if head:
        n_pad = LANES // 2
        q = q_ref[pl.ds(0, head), :].astype(F32) * scale
        q = jnp.concatenate([q, jnp.zeros((n_pad - head, hw), F32)], axis=0)
        first = lax.broadcasted_iota(jnp.int32, (hw, n_pad), 0) < hd
        qt = q.T
        qt = jnp.concatenate([jnp.where(first, qt, 0.0), jnp.where(first, 0.0, qt)],
                             axis=1).astype(BF16)
        s = jnp.dot(kh, qt, preferred_element_type=F32)
        key_h = lax.broadcasted_iota(jnp.int32, (head, 2 * n_pad), 0)
        qry_h = lax.broadcasted_iota(jnp.int32, (head, 2 * n_pad), 1)
        qry_h = jnp.where(qry_h >= n_pad, qry_h - n_pad, qry_h)
        s = jnp.where(key_h <= qry_h, s + slope * key_h.astype(F32), -jnp.inf)
        p = jnp.exp2(s - jnp.max(s, axis=0, keepdims=True))
        l = jnp.sum(p, axis=0, keepdims=True)
        acc = _dot_tn(v_ref[pl.ds(0, head), :], p.astype(BF16))
        o = finish_rows(acc, l, n_pad)
        o_ref[pl.ds(0, head), :] = o[:head].astype(o_ref.dtype)


def _da_diag_mask(block, head):
    key = np.arange(block + head)[:, None]
    qry = np.arange(2 * block)[None, :] % block
    visible = (key <= qry) | (key >= block)
    return jnp.asarray(np.where(visible, 0.0, -np.inf).astype(np.float32))


def da_attention(qkv, lam_q1, lam_k1, lam_q2, lam_k2, subln_g, *, lambda_init,
                 block=ATTN_BLOCK):
    b, seq_len, three_d = qkv.shape
    d = three_d // 3
    hw = 2 * DA_HEAD_DIM
    n_heads = d // hw
    n_blocks = seq_len // block
    n_off = n_blocks * (n_blocks - 1) // 2
    head = seq_len - n_blocks * block
    assert n_off % STEP_UNROLL == 0 and head <= LANES // 2 and head % BF16_SUBLANES == 0
    assert block <= 256
    qi_tab, kj_tab = _da_schedule(n_blocks)
    slopes = jnp.asarray(
        np.array([2.0 ** (-8.0 * (i + 1) / n_heads) for i in range(n_heads)],
                 dtype=np.float32))
    vec = lambda a: a.reshape(1, -1).astype(F32)
    small = lambda n: pl.BlockSpec((1, n), lambda bi, hi: (0, 0))
    smem = pl.BlockSpec(memory_space=pltpu.SMEM)
    width = 2 * block
    rows_d = block + head
    return pl.pallas_call(
        functools.partial(_da_attn_kernel, seq_len=seq_len, block=block,
                          lambda_init=lambda_init),
        out_shape=jax.ShapeDtypeStruct((b, seq_len, d), BF16),
        grid=(b, n_heads),
        in_specs=[
            smem, smem, smem,
            small(DA_HEAD_DIM), small(DA_HEAD_DIM), small(DA_HEAD_DIM),
            small(DA_HEAD_DIM),
            pl.BlockSpec((None, seq_len, hw), lambda bi, hi: (bi, 0, hi)),
            pl.BlockSpec((None, seq_len, hw), lambda bi, hi: (bi, 0, n_heads + hi)),
            pl.BlockSpec((None, seq_len, hw),
                         lambda bi, hi: (bi, 0, 2 * n_heads + hi)),
            small(hw),
            pl.BlockSpec((rows_d, width), lambda bi, hi: (0, 0)),
        ],
        out_specs=pl.BlockSpec((None, seq_len, hw), lambda bi, hi: (bi, 0, hi)),
        scratch_shapes=[
            pltpu.VMEM((n_blocks, hw, width), BF16),
            pltpu.VMEM((n_blocks, block, hw), BF16),
            pltpu.VMEM((n_blocks, block, hw), BF16),
            pltpu.VMEM((n_blocks, hw + ONES_ROWS, block), BF16),
            pltpu.VMEM((rows_d, width), F32),
            pltpu.VMEM((rows_d, width), F32),
            pltpu.VMEM((rows_d, width), BF16),
            pltpu.VMEM((rows_d, width), BF16),
            pltpu.VMEM((n_blocks + 1, hw + ONES_ROWS, width), F32),
            pltpu.VMEM((n_blocks, 1, width), F32),
        ],
        compiler_params=pltpu.CompilerParams(
            dimension_semantics=("parallel", "parallel"),
            vmem_limit_bytes=VMEM_LIMIT_BYTES),
        name="da_attention",
    )(qi_tab, kj_tab, slopes, vec(lam_q1), vec(lam_k1), vec(lam_q2), vec(lam_k2),
      qkv, qkv, qkv, vec(subln_g), _da_diag_mask(block, head))


MASKED = 1e30


def interleave_key_rows(x, head, block):
    b, seq_len, c = x.shape
    n_blocks = (seq_len - head) // block
    body = x[:, head:].reshape(b, n_blocks, SUBLANES, block // SUBLANES, c)
    body = jnp.swapaxes(body, 2, 3).reshape(b, n_blocks * block, c)
    return x[:, :head], body


def _neg_abs(x):
    sign = jnp.int32(-2 ** 31)
    return lax.bitcast_convert_type(lax.bitcast_convert_type(x, jnp.int32) | sign, F32)


def _log2_sigmoid_pair(z2):
    sp = jnp.log2(1.0 + jnp.exp2(_neg_abs(z2)))
    log_beta = jnp.minimum(z2, 0.0) - sp
    return log_beta, log_beta - z2


def _sb_attn_kernel(qi_tab, kj_tab, q_ref, kh_ref, vh_ref, k_ref, v_ref, o_ref,
                    qt_ref, s_a, s_b, p_a, p_b, e_ref, fin_ref, rest_ref, mask_ref,
                    *, seq_len, block):
    hw = SB_HEAD_DIM
    nq = block
    n_blocks = seq_len // block
    head = seq_len - n_blocks * block
    n_steps = n_blocks * (n_blocks + 1) // 2
    groups = block // SUBLANES

    row = lax.broadcasted_iota(jnp.int32, (block, nq), 0)
    qry = lax.broadcasted_iota(jnp.int32, (block, nq), 1)
    key = (row >> int(math.log2(SUBLANES))) + groups * (row & (SUBLANES - 1))
    mask_ref[...] = jnp.where(key < qry, 0.0, -MASKED)

    for qi in range(n_blocks):
        q0 = head + qi * block
        qt_ref[qi] = q_ref[pl.ds(q0, nq), :].astype(F32).T.astype(BF16)
        rest_ref[qi] = jnp.ones((1, nq), F32)
        fin_ref[qi] = jnp.zeros((hw, nq), F32)
    fin_ref[n_blocks] = jnp.zeros((hw, nq), F32)

    def key_rows(kj):
        return pl.ds(pl.multiple_of(kj * block, block), block)

    def scores(t, s_out):
        s_out[...] = jnp.dot(k_ref[key_rows(kj_tab[t + 2]), :], qt_ref[qi_tab[t + 2]],
                             preferred_element_type=F32)

    def add_values(t, p_in):
        qi = qi_tab[t + 2]
        fin_ref[qi] = fin_ref[qi] + _dot_tn(v_ref[key_rows(kj_tab[t + 2]), :], p_in[...])

    sub = lax.broadcasted_iota(jnp.int32, (SUBLANES, nq), 0)

    def step(t, s_cur, s_next, p_buf, diagonal):
        scores(t + 1, s_next)
        add_values(t - 2, p_buf)
        qi = qi_tab[t + 2]
        rest = rest_ref[qi]
        run = jnp.ones((SUBLANES, nq), F32)
        for r in reversed(range(groups)):
            rows_r = pl.ds(r * SUBLANES, SUBLANES)
            z = s_cur[rows_r, :]
            if diagonal:
                z = z + mask_ref[rows_r, :]
            e = jnp.exp2(_neg_abs(z))
            big = run / (1.0 + e)
            small = big * e
            positive = z >= 0.0
            e_ref[rows_r, :] = jnp.where(positive, big, small)
            run = jnp.where(positive, small, big)
        above = jnp.ones((SUBLANES, nq), F32)
        for d in range(1, SUBLANES):
            shifted = pltpu.roll(run, SUBLANES - d, 0)
            above = above * jnp.where(sub < SUBLANES - d, shifted, 1.0)
        factor = above * rest
        for r in range(groups):
            rows_r = pl.ds(r * SUBLANES, SUBLANES)
            p_buf[rows_r, :] = (e_ref[rows_r, :] * factor).astype(BF16)
        rest_ref[qi] = rest * (above * run)[0:1, :]

    scores(0, s_a)
    p_a[...] = jnp.zeros((block, nq), BF16)
    p_b[...] = jnp.zeros((block, nq), BF16)

    bufs = ((s_a, s_b, p_a), (s_b, s_a, p_b))
    for t in range(n_blocks):
        step(t, *bufs[t % 2], True)

    def quad_step(u, _):
        for i in range(STEP_UNROLL):
            step(n_blocks + STEP_UNROLL * u + i, *bufs[(n_blocks + i) % 2], False)
        return 0

    lax.fori_loop(0, (n_steps - n_blocks) // STEP_UNROLL, quad_step, 0)
    add_values(n_steps - 2, p_a)
    add_values(n_steps - 1, p_b)

    def suffix_matrix(n):
        r = lax.broadcasted_iota(jnp.int32, (n, n), 0)
        c = lax.broadcasted_iota(jnp.int32, (n, n), 1)
        return jnp.where(c > r, 1.0, 0.0).astype(BF16)

    def later_sum(x, tri):
        hi = x.astype(BF16)
        lo = (x - hi.astype(F32)).astype(BF16)
        return (jnp.dot(tri, hi, preferred_element_type=F32)
                + jnp.dot(tri, lo, preferred_element_type=F32))

    tri = suffix_matrix(head)
    kh = kh_ref[...]
    vh = vh_ref[...]

    for qi in range(n_blocks):
        z = jnp.dot(kh, qt_ref[qi], preferred_element_type=F32)
        log_beta, log_1m = _log2_sigmoid_pair(z)
        attn = jnp.exp2(log_beta + later_sum(log_1m, tri)) * rest_ref[qi]
        acc = fin_ref[qi] + _dot_tn(vh, attn.astype(BF16))
        o_ref[pl.ds(head + qi * block, nq), :] = acc.T.astype(o_ref.dtype)

    qt = q_ref[pl.ds(0, head), :].astype(F32)
    qt = jnp.concatenate([qt, jnp.zeros((LANES - head, hw), F32)], axis=0)
    qt = qt.T.astype(BF16)
    z = jnp.dot(kh, qt, preferred_element_type=F32)
    key_h = lax.broadcasted_iota(jnp.int32, (head, LANES), 0)
    qry_h = lax.broadcasted_iota(jnp.int32, (head, LANES), 1)
    z = jnp.where(key_h < qry_h, z, -MASKED)
    log_beta, log_1m = _log2_sigmoid_pair(z)
    attn = jnp.exp2(log_beta + later_sum(log_1m, tri))
    acc = _dot_tn(vh, attn.astype(BF16))
    o_ref[pl.ds(0, head), :] = acc.T[:head].astype(o_ref.dtype)


def sb_attention(q, kv_head, kv_body, *, block=ATTN_BLOCK):
    b, seq_len, d = q.shape
    hw = SB_HEAD_DIM
    n_heads = d // hw
    n_blocks = seq_len // block
    head = seq_len - n_blocks * block
    n_steps = n_blocks * (n_blocks + 1) // 2
    assert (n_steps - n_blocks) % STEP_UNROLL == 0 and n_blocks % 2 == 0
    assert 0 < head <= LANES and head % BF16_SUBLANES == 0
    qi_tab, kj_tab = _sb_schedule(n_blocks)
    smem = pl.BlockSpec(memory_space=pltpu.SMEM)
    body_len = n_blocks * block
    return pl.pallas_call(
        functools.partial(_sb_attn_kernel, seq_len=seq_len, block=block),
        out_shape=jax.ShapeDtypeStruct((b, seq_len, d), BF16),
        grid=(b, n_heads),
        in_specs=[
            smem, smem,
            pl.BlockSpec((None, seq_len, hw), lambda bi, hi: (bi, 0, hi)),
            pl.BlockSpec((None, head, hw), lambda bi, hi: (bi, 0, hi)),
            pl.BlockSpec((None, head, hw), lambda bi, hi: (bi, 0, n_heads + hi)),
            pl.BlockSpec((None, body_len, hw), lambda bi, hi: (bi, 0, hi)),
            pl.BlockSpec((None, body_len, hw), lambda bi, hi: (bi, 0, n_heads + hi)),
        ],
        out_specs=pl.BlockSpec((None, seq_len, hw), lambda bi, hi: (bi, 0, hi)),
        scratch_shapes=[
            pltpu.VMEM((n_blocks, hw, block), BF16),
            pltpu.VMEM((block, block), F32),
            pltpu.VMEM((block, block), F32),
            pltpu.VMEM((block, block), BF16),
            pltpu.VMEM((block, block), BF16),
            pltpu.VMEM((block, block), F32),
            pltpu.VMEM((n_blocks + 1, hw, block), F32),
            pltpu.VMEM((n_blocks, 1, block), F32),
            pltpu.VMEM((block, block), F32),
        ],
        compiler_params=pltpu.CompilerParams(
            dimension_semantics=("parallel", "parallel"),
            vmem_limit_bytes=VMEM_LIMIT_BYTES),
        name="sb_attention",
    )(qi_tab, kj_tab, q, kv_head, kv_head, kv_body, kv_body)


def kernel(x, meta_tokens, attn_norm_g, mlp_norm_g, da_w_qkv, da_w_o, da_lambda_q1,
           da_lambda_k1, da_lambda_q2, da_lambda_k2, da_subln_g, kv_norm_g, sb_w_k,
           sb_w_v, sb_w_q, sb_w_o, mlp_w_up, mlp_w_down, final_norm_g):
    b, seq, d = x.shape
    n_meta = meta_tokens.shape[0]
    depth = attn_norm_g.shape[0]
    n_a = da_w_qkv.shape[0]
    seq_len = n_meta + seq

    meta = jnp.broadcast_to(meta_tokens.astype(x.dtype)[None], (b, n_meta, d))
    h = jnp.concatenate([meta, x], axis=1).reshape(b * seq_len, d)

    kv = None
    for i in range(depth):
        if i < n_a:
            lambda_init = 0.8 - 0.6 * math.exp(-0.3 * i)
            qkv = norm_matmul(h, attn_norm_g[i], da_w_qkv[i].astype(BF16))
            o = da_attention(qkv.reshape(b, seq_len, 3 * d), da_lambda_q1[i],
                             da_lambda_k1[i], da_lambda_q2[i], da_lambda_k2[i],
                             da_subln_g[i], lambda_init=lambda_init)
            w_o = da_w_o[i]
        else:
            j = i - n_a
            if kv is None:
                w_kv = jnp.concatenate([sb_w_k, sb_w_v], axis=1).astype(BF16)
                kv = norm_matmul(h, kv_norm_g, w_kv).reshape(b, seq_len, 2 * d)
                kv_head, kv_body = interleave_key_rows(kv, seq_len % ATTN_BLOCK,
                                                       ATTN_BLOCK)
            q = norm_matmul(h, attn_norm_g[i], sb_w_q[j].astype(BF16),
                            out_scale=SB_HEAD_DIM ** -0.5 * LOG2_E)
            o = sb_attention(q.reshape(b, seq_len, d), kv_head, kv_body)
            w_o = sb_w_o[j]
        h = attn_out_mlp(h, o.reshape(b * seq_len, d), w_o.astype(BF16), mlp_norm_g[i],
                         mlp_w_up[i].astype(BF16), mlp_w_down[i].astype(BF16))
    return final_norm(h.reshape(b, seq_len, d), final_norm_g, n_meta)
```

```python
import functools
import math

import jax
import jax.numpy as jnp
import numpy as np
from jax import lax
from jax.experimental import pallas as pl
from jax.experimental.pallas import tpu as pltpu

F32 = jnp.float32
BF16 = jnp.bfloat16

RMS_EPS = 1e-6
LOG2_E = math.log2(math.e)
DA_HEAD_DIM = 64
SB_HEAD_DIM = 128
ATTN_BLOCK = 256
STEP_UNROLL = 4
SUBLANES = 8
BF16_SUBLANES = 16
LANES = 128
VMEM_LIMIT_BYTES = 56 * 1024 * 1024


def _pick_tile(n, target, multiple=BF16_SUBLANES):
    best = None
    for t in range(multiple, min(n, target) + 1, multiple):
        if n % t == 0:
            best = t
    if best is None:
        raise ValueError(f"no tile for {n=} {target=}")
    return best


def _rmsnorm_rows(x, g):
    ms = jnp.mean(x * x, axis=-1, keepdims=True)
    return x * lax.rsqrt(ms + RMS_EPS) * g


def _norm_matmul_kernel(h_ref, g_ref, w_ref, o_ref, xn_ref, *, out_scale):
    @pl.when(pl.program_id(1) == 0)
    def _():
        xn_ref[...] = _rmsnorm_rows(h_ref[...], g_ref[...]).astype(BF16)

    acc = jnp.dot(xn_ref[...], w_ref[...], preferred_element_type=F32)
    if out_scale != 1.0:
        acc = acc * out_scale
    o_ref[...] = acc.astype(o_ref.dtype)


def norm_matmul(h, g, w, *, out_scale=1.0, tm_target=768, tn_target=2048):
    m, d = h.shape
    n = w.shape[1]
    tm = _pick_tile(m, tm_target)
    tn = _pick_tile(n, tn_target, LANES)
    return pl.pallas_call(
        functools.partial(_norm_matmul_kernel, out_scale=out_scale),
        out_shape=jax.ShapeDtypeStruct((m, n), BF16),
        grid=(m // tm, n // tn),
        in_specs=[
            pl.BlockSpec((tm, d), lambda i, j: (i, 0)),
            pl.BlockSpec((1, d), lambda i, j: (0, 0)),
            pl.BlockSpec((d, tn), lambda i, j: (0, j)),
        ],
        out_specs=pl.BlockSpec((tm, tn), lambda i, j: (i, j)),
        scratch_shapes=[pltpu.VMEM((tm, d), BF16)],
        compiler_params=pltpu.CompilerParams(
            dimension_semantics=("parallel", "arbitrary"),
            vmem_limit_bytes=VMEM_LIMIT_BYTES),
        name="norm_matmul",
    )(h, g.reshape(1, d), w)


def _attn_out_mlp_kernel(h_ref, o_ref, wo_ref, g_ref, wu_ref, wd_ref, out_ref, xn_ref):
    @pl.when(pl.program_id(1) == 0)
    def _():
        x = h_ref[...] + jnp.dot(o_ref[...], wo_ref[...], preferred_element_type=F32)
        xn_ref[...] = _rmsnorm_rows(x, g_ref[...]).astype(BF16)
        out_ref[...] = x

    a = jnp.dot(xn_ref[...], wu_ref[...], preferred_element_type=F32)
    a = jnp.maximum(a, 0.0)
    a = (a * a).astype(BF16)
    out_ref[...] += jnp.dot(a, wd_ref[...], preferred_element_type=F32)


def attn_out_mlp(h, o, w_o, g, w_up, w_down, *, tm_target=768, tf_target=512):
    m, d = h.shape
    f = w_up.shape[1]
    tm = _pick_tile(m, tm_target)
    tf = _pick_tile(f, tf_target, LANES)
    return pl.pallas_call(
        _attn_out_mlp_kernel,
        out_shape=jax.ShapeDtypeStruct((m, d), F32),
        grid=(m // tm, f // tf),
        in_specs=[
            pl.BlockSpec((tm, d), lambda i, j: (i, 0)),
            pl.BlockSpec((tm, d), lambda i, j: (i, 0)),
            pl.BlockSpec((d, d), lambda i, j: (0, 0), pipeline_mode=pl.Buffered(1)),
            pl.BlockSpec((1, d), lambda i, j: (0, 0)),
            pl.BlockSpec((d, tf), lambda i, j: (0, j)),
            pl.BlockSpec((tf, d), lambda i, j: (j, 0)),
        ],
        out_specs=pl.BlockSpec((tm, d), lambda i, j: (i, 0)),
        scratch_shapes=[pltpu.VMEM((tm, d), BF16)],
        compiler_params=pltpu.CompilerParams(
            dimension_semantics=("parallel", "arbitrary"),
            vmem_limit_bytes=VMEM_LIMIT_BYTES),
        name="attn_out_mlp",
    )(h, o, w_o, g.reshape(1, d), w_up, w_down)


def _final_norm_kernel(h_ref, g_ref, o_ref):
    o_ref[...] = _rmsnorm_rows(h_ref[...], g_ref[...])


def final_norm(h, g, skip, *, tr_target=512):
    b, seq_len, d = h.shape
    rows = seq_len - skip
    assert skip % SUBLANES == 0
    tr = _pick_tile(rows, tr_target, SUBLANES)
    return pl.pallas_call(
        _final_norm_kernel,
        out_shape=jax.ShapeDtypeStruct((b, rows, d), F32),
        grid=(b, rows // tr),
        in_specs=[
            pl.BlockSpec((pl.Element(1), pl.Element(tr), pl.Element(d)),
                         lambda bi, i: (bi, pl.multiple_of(skip + i * tr, SUBLANES), 0)),
            pl.BlockSpec((1, 1, d), lambda bi, i: (0, 0, 0)),
        ],
        out_specs=pl.BlockSpec((1, tr, d), lambda bi, i: (bi, i, 0)),
        compiler_params=pltpu.CompilerParams(
            dimension_semantics=("parallel", "parallel"),
            vmem_limit_bytes=VMEM_LIMIT_BYTES),
        name="final_norm",
    )(h, g.reshape(1, 1, d))


def _dot_tn(a, b):
    return lax.dot_general(a, b, (((0,), (0,)), ((), ())),
                           preferred_element_type=F32)


def _sb_schedule(n_blocks):
    steps = [(qi, qi) for qi in range(n_blocks)]
    steps += [(qi, kj) for qi in range(n_blocks) for kj in range(qi - 1, -1, -1)]
    steps = [(n_blocks, 0)] * 2 + steps + [steps[-1]]
    qi_tab = np.array([s[0] for s in steps], np.int32)
    kj_tab = np.array([s[1] for s in steps], np.int32)
    return jnp.asarray(qi_tab), jnp.asarray(kj_tab)


N_SLOPE_TERMS = 3
ONES_ROWS = BF16_SUBLANES


def _da_schedule(n_blocks):
    steps = [(qi, kj) for qi in range(n_blocks) for kj in range(qi)]
    steps = [(n_blocks, 0)] + steps + [steps[-1]]
    qi_tab = np.array([s[0] for s in steps], np.int32)
    kj_tab = np.array([s[1] for s in steps], np.int32)
    return jnp.asarray(qi_tab), jnp.asarray(kj_tab)


def _da_attn_kernel(qi_tab, kj_tab, slopes_ref, lq1_ref, lk1_ref, lq2_ref, lk2_ref,
                    q_ref, k_ref, v_ref, g_ref, mask_ref, o_ref,
                    qt_ref, ke1_ref, ke2_ref, vt_ref, s_a, s_b, p_a, p_b, fin_ref, m_ref,
                    *, seq_len, block, lambda_init):
    hd = DA_HEAD_DIM
    hw = 2 * hd
    nq = block
    width = 2 * nq
    n_blocks = seq_len // block
    head = seq_len - n_blocks * block
    n_off = n_blocks * (n_blocks - 1) // 2
    rows_d = block + head
    slope = slopes_ref[pl.program_id(1)] * LOG2_E
    lam = (jnp.exp(jnp.sum(lq1_ref[...] * lk1_ref[...], axis=-1, keepdims=True))
           - jnp.exp(jnp.sum(lq2_ref[...] * lk2_ref[...], axis=-1, keepdims=True))
           + lambda_init)
    scale = hd ** -0.5 * LOG2_E
    out_gain = g_ref[...] * (1.0 - lambda_init)

    def to_f32(i):
        return jnp.asarray(i, jnp.int32).astype(F32)

    sv = jnp.full((1, nq), slope, F32)
    terms = []
    rem = sv
    for _ in range(N_SLOPE_TERMS):
        t_bf = rem.astype(BF16).astype(F32)
        terms.append(t_bf)
        rem = rem - t_bf
    rows = terms + [t * float(block) for t in terms]
    feat_row = lax.broadcasted_iota(jnp.int32, (hw, nq), 0)

    def slope_tile(base):
        tile = jnp.zeros((hw, nq), F32)
        for i, r in enumerate(rows):
            tile = jnp.where(feat_row == base + i, r, tile)
        return tile

    first_map = feat_row < hd
    slopes_1 = slope_tile(hd)
    slopes_2 = slope_tile(0)

    def key_features(n, base, offset, shift):
        lane = lax.broadcasted_iota(jnp.int32, (n, hw), 1) - base
        pos = lax.broadcasted_iota(jnp.int32, (n, hw), 0) + offset
        f = jnp.where((lane >= 0) & (lane < N_SLOPE_TERMS), pos, 0)
        f = jnp.where((lane >= N_SLOPE_TERMS) & (lane < 2 * N_SLOPE_TERMS), shift, f)
        return f.astype(F32).astype(BF16)

    key_lane = lax.broadcasted_iota(jnp.int32, (block, hw), 1)
    feat_1 = key_features(block, hd, 0, 0)
    feat_2 = key_features(block, 0, 0, 0)
    ones_rows = jnp.ones((ONES_ROWS, block), BF16)
    for kb in range(n_blocks):
        r0 = head + kb * block
        k = k_ref[pl.ds(r0, block), :]
        ke1_ref[kb] = jnp.where(key_lane < hd, k, feat_1)
        ke2_ref[kb] = jnp.where(key_lane < hd, feat_2, k)
        vt_ref[kb, pl.ds(0, hw), :] = v_ref[pl.ds(r0, block), :].astype(F32).T.astype(BF16)
        vt_ref[kb, pl.ds(hw, ONES_ROWS), :] = ones_rows

    for qi in range(n_blocks):
        q0 = head + qi * block
        qt = (q_ref[pl.ds(q0, nq), :].astype(F32) * scale).T
        qt_ref[qi] = jnp.concatenate([jnp.where(first_map, qt, slopes_1),
                                      jnp.where(first_map, slopes_2, qt)],
                                     axis=1).astype(BF16)
        m_ref[qi] = jnp.full((1, width), -jnp.inf, F32)
        fin_ref[qi] = jnp.zeros((hw + ONES_ROWS, width), F32)
    fin_ref[n_blocks] = jnp.zeros((hw + ONES_ROWS, width), F32)

    def two_map_scores(k1, k2, qt):
        return jnp.concatenate(
            [jnp.dot(k1, qt[:, :nq], preferred_element_type=F32),
             jnp.dot(k2, qt[:, nq:], preferred_element_type=F32)], axis=1)

    def softmax_update(qi, s, c, p_out):
        m_old = m_ref[qi]
        m_new = jnp.maximum(m_old, jnp.max(s, axis=0, keepdims=True) + c)
        alpha = jnp.exp2(m_old - m_new)
        p_out[...] = jnp.exp2(s - (m_new - c)).astype(BF16)
        m_ref[qi] = m_new
        fin_ref[qi] = alpha * fin_ref[qi]

    top = pl.ds(0, block)

    def scores(t, s_out):
        kj = kj_tab[t + 1]
        s_out[top, :] = two_map_scores(ke1_ref[kj], ke2_ref[kj], qt_ref[qi_tab[t + 1]])

    def add_values(t, p_in):
        qi = qi_tab[t + 1]
        fin_ref[qi] = fin_ref[qi] + jnp.dot(vt_ref[kj_tab[t + 1]], p_in[top, :],
                                            preferred_element_type=F32)

    def step(t, s_cur, s_next, p_prev, p_cur):
        scores(t + 1, s_next)
        add_values(t - 1, p_prev)
        qi = qi_tab[t + 1]
        kj = kj_tab[t + 1]
        c = slope * to_f32((kj - qi) * block)
        softmax_update(qi, s_cur[top, :], c, p_cur.at[top])

    scores(0, s_a)
    p_b[...] = jnp.zeros((rows_d, width), BF16)

    def quad_step(u, _):
        step(4 * u, s_a, s_b, p_b, p_a)
        step(4 * u + 1, s_b, s_a, p_a, p_b)
        step(4 * u + 2, s_a, s_b, p_b, p_a)
        step(4 * u + 3, s_b, s_a, p_a, p_b)
        return 0

    lax.fori_loop(0, n_off // STEP_UNROLL, quad_step, 0)

    if head:
        kh = k_ref[pl.ds(0, head), :]
        head_lane = lax.broadcasted_iota(jnp.int32, (head, hw), 1)
        vh = jnp.concatenate([v_ref[pl.ds(0, head), :].astype(F32),
                              jnp.zeros((LANES - head, hw), F32)], axis=0)
        vt_head = jnp.concatenate([vh.T[:, :head].astype(BF16),
                                   jnp.ones((ONES_ROWS, head), BF16)], axis=0)

    def diag_scores(qi, s_out):
        k1, k2 = ke1_ref[qi], ke2_ref[qi]
        if head:
            f1 = key_features(head, hd, block - head, -(qi + 1))
            f2 = key_features(head, 0, block - head, -(qi + 1))
            k1 = jnp.concatenate([k1, jnp.where(head_lane < hd, kh, f1)], axis=0)
            k2 = jnp.concatenate([k2, jnp.where(head_lane < hd, f2, kh)], axis=0)
        s_out[...] = two_map_scores(k1, k2, qt_ref[qi])

    def diag_add_values(qi, p_in):
        vt = vt_ref[qi]
        if head:
            vt = jnp.concatenate([vt, vt_head], axis=1)
        fin_ref[qi] = fin_ref[qi] + jnp.dot(vt, p_in[...], preferred_element_type=F32)

    bufs = ((s_a, p_a), (s_b, p_b))
    diag_scores(0, s_a)
    for qi in range(n_blocks):
        (s_cur, p_cur), (s_next, p_prev) = bufs[qi % 2], bufs[(qi + 1) % 2]
        if qi + 1 < n_blocks:
            diag_scores(qi + 1, s_next)
        if qi == 0:
            add_values(n_off - 1, p_prev)
        else:
            diag_add_values(qi - 1, p_prev)
        softmax_update(qi, s_cur[...] + mask_ref[...], 0.0, p_cur)
    diag_add_values(n_blocks - 1, bufs[(n_blocks - 1) % 2][1])

    def finish_rows(acc, l, nrows):
        inv = 1.0 / l
        o = acc[:, :nrows] * inv[:, :nrows] - acc[:, nrows:] * (lam * inv[:, nrows:])
        ms = jnp.mean(o * o, axis=0, keepdims=True)
        return (o * lax.rsqrt(ms + RMS_EPS)).T * out_gain

    for qi in range(n_blocks):
        acc = fin_ref[qi]
        o = finish_rows(acc[:hw], acc[hw:hw + 1], nq)
        o_ref[pl.ds(head + qi * block, nq), :] = o.astype(o_ref.dtype)

    if head:
        n_pad = LANES // 2
        q = q_ref[pl.ds(0, head), :].astype(F32) * scale
        q = jnp.concatenate([q, jnp.zeros((n_pad - head, hw), F32)], axis=0)
        first = lax.broadcasted_iota(jnp.int32, (hw, n_pad), 0) < hd
        qt = q.T
        qt = jnp.concatenate([jnp.where(first, qt, 0.0), jnp.where(first, 0.0, qt)],
                             axis=1).astype(BF16)
        s = jnp.dot(kh, qt, preferred_element_type=F32)
        key_h = lax.broadcasted_iota(jnp.int32, (head, 2 * n_pad), 0)
        qry_h = lax.broadcasted_iota(jnp.int32, (head, 2 * n_pad), 1)
        qry_h = jnp.where(qry_h >= n_pad, qry_h - n_pad, qry_h)
        s = jnp.where(key_h <= qry_h, s + slope * key_h.astype(F32), -jnp.inf)
        p = jnp.exp2(s - jnp.max(s, axis=0, keepdims=True))
        l = jnp.sum(p, axis=0, keepdims=True)
        acc = _dot_tn(v_ref[pl.ds(0, head), :], p.astype(BF16))
        o = finish_rows(acc, l, n_pad)
        o_ref[pl.ds(0, head), :] = o[:head].astype(o_ref.dtype)


def _da_diag_mask(block, head):
    key = np.arange(block + head)[:, None]
    qry = np.arange(2 * block)[None, :] % block
    visible = (key <= qry) | (key >= block)
    return jnp.asarray(np.where(visible, 0.0, -np.inf).astype(np.float32))


def da_attention(qkv, lam_q1, lam_k1, lam_q2, lam_k2, subln_g, *, lambda_init,
                 block=ATTN_BLOCK):
    b, seq_len, three_d = qkv.shape
    d = three_d // 3
    hw = 2 * DA_HEAD_DIM
    n_heads = d // hw
    n_blocks = seq_len // block
    n_off = n_blocks * (n_blocks - 1) // 2
    head = seq_len - n_blocks * block
    assert n_off % STEP_UNROLL == 0 and head <= LANES // 2 and head % BF16_SUBLANES == 0
    assert block <= 256
    qi_tab, kj_tab = _da_schedule(n_blocks)
    slopes = jnp.asarray(
        np.array([2.0 ** (-8.0 * (i + 1) / n_heads) for i in range(n_heads)],
                 dtype=np.float32))
    vec = lambda a: a.reshape(1, -1).astype(F32)
    small = lambda n: pl.BlockSpec((1, n), lambda bi, hi: (0, 0))
    smem = pl.BlockSpec(memory_space=pltpu.SMEM)
    width = 2 * block
    rows_d = block + head
    return pl.pallas_call(
        functools.partial(_da_attn_kernel, seq_len=seq_len, block=block,
                          lambda_init=lambda_init),
        out_shape=jax.ShapeDtypeStruct((b, seq_len, d), BF16),
        grid=(b, n_heads),
        in_specs=[
            smem, smem, smem,
            small(DA_HEAD_DIM), small(DA_HEAD_DIM), small(DA_HEAD_DIM),
            small(DA_HEAD_DIM),
            pl.BlockSpec((None, seq_len, hw), lambda bi, hi: (bi, 0, hi)),
            pl.BlockSpec((None, seq_len, hw), lambda bi, hi: (bi, 0, n_heads + hi)),
            pl.BlockSpec((None, seq_len, hw),
                         lambda bi, hi: (bi, 0, 2 * n_heads + hi)),
            small(hw),
            pl.BlockSpec((rows_d, width), lambda bi, hi: (0, 0)),
        ],
        out_specs=pl.BlockSpec((None, seq_len, hw), lambda bi, hi: (bi, 0, hi)),
        scratch_shapes=[
            pltpu.VMEM((n_blocks, hw, width), BF16),
            pltpu.VMEM((n_blocks, block, hw), BF16),
            pltpu.VMEM((n_blocks, block, hw), BF16),
            pltpu.VMEM((n_blocks, hw + ONES_ROWS, block), BF16),
            pltpu.VMEM((rows_d, width), F32),
            pltpu.VMEM((rows_d, width), F32),
            pltpu.VMEM((rows_d, width), BF16),
            pltpu.VMEM((rows_d, width), BF16),
            pltpu.VMEM((n_blocks + 1, hw + ONES_ROWS, width), F32),
            pltpu.VMEM((n_blocks, 1, width), F32),
        ],
        compiler_params=pltpu.CompilerParams(
            dimension_semantics=("parallel", "parallel"),
            vmem_limit_bytes=VMEM_LIMIT_BYTES),
        name="da_attention",
    )(qi_tab, kj_tab, slopes, vec(lam_q1), vec(lam_k1), vec(lam_q2), vec(lam_k2),
      qkv, qkv, qkv, vec(subln_g), _da_diag_mask(block, head))


MASKED = 1e30
EXP2_CLAMP = 100.0


def interleave_key_rows(x, head, block):
    b, seq_len, c = x.shape
    n_blocks = (seq_len - head) // block
    body = x[:, head:].reshape(b, n_blocks, SUBLANES, block // SUBLANES, c)
    body = jnp.swapaxes(body, 2, 3).reshape(b, n_blocks * block, c)
    return x[:, :head], body


def _neg_abs(x):
    sign = jnp.int32(-2 ** 31)
    return lax.bitcast_convert_type(lax.bitcast_convert_type(x, jnp.int32) | sign, F32)


def _log2_sigmoid_pair(z2):
    sp = jnp.log2(1.0 + jnp.exp2(_neg_abs(z2)))
    log_beta = jnp.minimum(z2, 0.0) - sp
    return log_beta, log_beta - z2


def _sb_attn_kernel(qi_tab, kj_tab, q_ref, kh_ref, vh_ref, k_ref, v_ref, o_ref,
                    qt_ref, s_a, s_b, p_a, p_b, e_ref, fin_ref, rest_ref, mask_ref,
                    *, seq_len, block):
    hw = SB_HEAD_DIM
    nq = block
    n_blocks = seq_len // block
    head = seq_len - n_blocks * block
    n_steps = n_blocks * (n_blocks + 1) // 2
    groups = block // SUBLANES

    row = lax.broadcasted_iota(jnp.int32, (block, nq), 0)
    qry = lax.broadcasted_iota(jnp.int32, (block, nq), 1)
    key = (row >> int(math.log2(SUBLANES))) + groups * (row & (SUBLANES - 1))
    mask_ref[...] = jnp.where(key < qry, 0.0, MASKED)

    for qi in range(n_blocks):
        q0 = head + qi * block
        qt_ref[qi] = q_ref[pl.ds(q0, nq), :].T
        rest_ref[qi] = jnp.ones((1, nq), F32)
        fin_ref[qi] = jnp.zeros((hw, nq), F32)
    fin_ref[n_blocks] = jnp.zeros((hw, nq), F32)

    def key_rows(kj):
        return pl.ds(pl.multiple_of(kj * block, block), block)

    def scores(t, s_out):
        s_out[...] = jnp.dot(k_ref[key_rows(kj_tab[t + 2]), :], qt_ref[qi_tab[t + 2]],
                             preferred_element_type=F32)

    def add_values(t, p_in):
        qi = qi_tab[t + 2]
        fin_ref[qi] = fin_ref[qi] + _dot_tn(v_ref[key_rows(kj_tab[t + 2]), :], p_in[...])

    sub = lax.broadcasted_iota(jnp.int32, (SUBLANES, nq), 0)

    def step(t, s_cur, s_next, p_buf, diagonal):
        scores(t + 1, s_next)
        add_values(t - 2, p_buf)
        qi = qi_tab[t + 2]
        rest = rest_ref[qi]
        run = jnp.ones((SUBLANES, nq), F32)
        for r in reversed(range(groups)):
            rows_r = pl.ds(r * SUBLANES, SUBLANES)
            zn = s_cur[rows_r, :]
            if diagonal:
                hidden = mask_ref[rows_r, :]
                zn = zn + hidden
            e = jnp.exp2(jnp.minimum(zn, EXP2_CLAMP))
            beta = 1.0 / (1.0 + e)
            beta_run = run * beta
            run = run * (e * beta)
            if diagonal:
                beta_run = jnp.where(hidden == 0.0, beta_run, 0.0)
            e_ref[rows_r, :] = beta_run
        above = jnp.ones((SUBLANES, nq), F32)
        for d in range(1, SUBLANES):
            shifted = pltpu.roll(run, SUBLANES - d, 0)
            above = above * jnp.where(sub < SUBLANES - d, shifted, 1.0)
        factor = above * rest
        for r in range(groups):
            rows_r = pl.ds(r * SUBLANES, SUBLANES)
            p_buf[rows_r, :] = (e_ref[rows_r, :] * factor).astype(BF16)
        rest_ref[qi] = rest * (above * run)[0:1, :]

    scores(0, s_a)
    p_a[...] = jnp.zeros((block, nq), BF16)
    p_b[...] = jnp.zeros((block, nq), BF16)

    bufs = ((s_a, s_b, p_a), (s_b, s_a, p_b))
    for t in range(n_blocks):
        step(t, *bufs[t % 2], True)

    def quad_step(u, _):
        for i in range(STEP_UNROLL):
            step(n_blocks + STEP_UNROLL * u + i, *bufs[(n_blocks + i) % 2], False)
        return 0

    lax.fori_loop(0, (n_steps - n_blocks) // STEP_UNROLL, quad_step, 0)
    add_values(n_steps - 2, p_a)
    add_values(n_steps - 1, p_b)

    def suffix_matrix(n):
        r = lax.broadcasted_iota(jnp.int32, (n, n), 0)
        c = lax.broadcasted_iota(jnp.int32, (n, n), 1)
        return jnp.where(c > r, 1.0, 0.0).astype(BF16)

    def later_sum(x, tri):
        hi = x.astype(BF16)
        lo = (x - hi.astype(F32)).astype(BF16)
        return (jnp.dot(tri, hi, preferred_element_type=F32)
                + jnp.dot(tri, lo, preferred_element_type=F32))

    tri = suffix_matrix(head)
    kh = kh_ref[...]
    vh = vh_ref[...]

    qt_all = jnp.concatenate([qt_ref[qi] for qi in range(n_blocks)], axis=1)
    rest_all = jnp.concatenate([rest_ref[qi] for qi in range(n_blocks)], axis=1)
    zn = jnp.dot(kh, qt_all, preferred_element_type=F32)
    log_beta, log_1m = _log2_sigmoid_pair(-zn)
    attn = jnp.exp2(log_beta + later_sum(log_1m, tri)) * rest_all
    pv_head = _dot_tn(vh, attn.astype(BF16))
    for qi in range(n_blocks):
        acc = fin_ref[qi] + pv_head[:, qi * nq:(qi + 1) * nq]
        o_ref[pl.ds(head + qi * block, nq), :] = acc.T.astype(o_ref.dtype)

    qt = q_ref[pl.ds(0, head), :].astype(F32)
    qt = jnp.concatenate([qt, jnp.zeros((LANES - head, hw), F32)], axis=0)
    qt = qt.T.astype(BF16)
    zn = jnp.dot(kh, qt, preferred_element_type=F32)
    key_h = lax.broadcasted_iota(jnp.int32, (head, LANES), 0)
    qry_h = lax.broadcasted_iota(jnp.int32, (head, LANES), 1)
    z = jnp.where(key_h < qry_h, -zn, -MASKED)
    log_beta, log_1m = _log2_sigmoid_pair(z)
    attn = jnp.exp2(log_beta + later_sum(log_1m, tri))
    acc = _dot_tn(vh, attn.astype(BF16))
    o_ref[pl.ds(0, head), :] = acc.T[:head].astype(o_ref.dtype)


def sb_attention(q, kv_head, kv_body, *, block=ATTN_BLOCK):
    b, seq_len, d = q.shape
    hw = SB_HEAD_DIM
    n_heads = d // hw
    n_blocks = seq_len // block
    head = seq_len - n_blocks * block
    n_steps = n_blocks * (n_blocks + 1) // 2
    assert (n_steps - n_blocks) % STEP_UNROLL == 0 and n_blocks % 2 == 0
    assert 0 < head <= LANES and head % BF16_SUBLANES == 0
    qi_tab, kj_tab = _sb_schedule(n_blocks)
    smem = pl.BlockSpec(memory_space=pltpu.SMEM)
    body_len = n_blocks * block
    return pl.pallas_call(
        functools.partial(_sb_attn_kernel, seq_len=seq_len, block=block),
        out_shape=jax.ShapeDtypeStruct((b, seq_len, d), BF16),
        grid=(b, n_heads),
        in_specs=[
            smem, smem,
            pl.BlockSpec((None, seq_len, hw), lambda bi, hi: (bi, 0, hi)),
            pl.BlockSpec((None, head, hw), lambda bi, hi: (bi, 0, hi)),
            pl.BlockSpec((None, head, hw), lambda bi, hi: (bi, 0, n_heads + hi)),
            pl.BlockSpec((None, body_len, hw), lambda bi, hi: (bi, 0, hi)),
            pl.BlockSpec((None, body_len, hw), lambda bi, hi: (bi, 0, n_heads + hi)),
        ],
        out_specs=pl.BlockSpec((None, seq_len, hw), lambda bi, hi: (bi, 0, hi)),
        scratch_shapes=[
            pltpu.VMEM((n_blocks, hw, block), BF16),
            pltpu.VMEM((block, block), F32),
            pltpu.VMEM((block, block), F32),
            pltpu.VMEM((block, block), BF16),
            pltpu.VMEM((block, block), BF16),
            pltpu.VMEM((block, block), F32),
            pltpu.VMEM((n_blocks + 1, hw, block), F32),
            pltpu.VMEM((n_blocks, 1, block), F32),
            pltpu.VMEM((block, block), F32),
        ],
        compiler_params=pltpu.CompilerParams(
            dimension_semantics=("parallel", "parallel"),
            vmem_limit_bytes=VMEM_LIMIT_BYTES),
        name="sb_attention",
    )(qi_tab, kj_tab, q, kv_head, kv_head, kv_body, kv_body)


def kernel(x, meta_tokens, attn_norm_g, mlp_norm_g, da_w_qkv, da_w_o, da_lambda_q1,
           da_lambda_k1, da_lambda_q2, da_lambda_k2, da_subln_g, kv_norm_g, sb_w_k,
           sb_w_v, sb_w_q, sb_w_o, mlp_w_up, mlp_w_down, final_norm_g):
    b, seq, d = x.shape
    n_meta = meta_tokens.shape[0]
    depth = attn_norm_g.shape[0]
    n_a = da_w_qkv.shape[0]
    seq_len = n_meta + seq

    meta = jnp.broadcast_to(meta_tokens.astype(x.dtype)[None], (b, n_meta, d))
    h = jnp.concatenate([meta, x], axis=1).reshape(b * seq_len, d)

    kv = None
    for i in range(depth):
        if i < n_a:
            lambda_init = 0.8 - 0.6 * math.exp(-0.3 * i)
            qkv = norm_matmul(h, attn_norm_g[i], da_w_qkv[i].astype(BF16))
            o = da_attention(qkv.reshape(b, seq_len, 3 * d), da_lambda_q1[i],
                             da_lambda_k1[i], da_lambda_q2[i], da_lambda_k2[i],
                             da_subln_g[i], lambda_init=lambda_init)
            w_o = da_w_o[i]
        else:
            j = i - n_a
            if kv is None:
                w_kv = jnp.concatenate([sb_w_k, sb_w_v], axis=1).astype(BF16)
                kv = norm_matmul(h, kv_norm_g, w_kv).reshape(b, seq_len, 2 * d)
                kv_head, kv_body = interleave_key_rows(kv, seq_len % ATTN_BLOCK,
                                                       ATTN_BLOCK)
            q = norm_matmul(h, attn_norm_g[i], sb_w_q[j].astype(BF16),
                            out_scale=-(SB_HEAD_DIM ** -0.5) * LOG2_E)
            o = sb_attention(q.reshape(b, seq_len, d), kv_head, kv_body)
            w_o = sb_w_o[j]
        h = attn_out_mlp(h, o.reshape(b * seq_len, d), w_o.astype(BF16), mlp_norm_g[i],
                         mlp_w_up[i].astype(BF16), mlp_w_down[i].astype(BF16))
    return final_norm(h.reshape(b, seq_len, d), final_norm_g, n_meta)
```
